```python
import math
import jax, jax.numpy as jnp
from jax import lax
import numpy as np

D_MODEL = 1024
BATCH = 8
SEQ = 4096
DEPTH = 1

MIX_WIDTH = 2 * D_MODEL
SSD_WIDTH = MIX_WIDTH // 2
SSD_HEAD_DIM = 64
SSD_HEADS = SSD_WIDTH // SSD_HEAD_DIM
SSD_GROUPS = 2
SSD_STATE = 128
SSD_CONV = 4
CHUNK = 128
CONF_WIDTH = MIX_WIDTH - SSD_WIDTH
CONF_KERNEL = 31
D_FF = 4 * D_MODEL
PLE_DIM = 256
EPS = 1e-6
XBC_WIDTH = SSD_WIDTH + 2 * SSD_GROUPS * SSD_STATE
IN_WIDTH = SSD_WIDTH + XBC_WIDTH + SSD_HEADS + 2 * CONF_WIDTH

kernel_name = "hybrid_ssd_conformer_block"


def rmsnorm(x, g):
    xf = x.astype(jnp.float32)
    y = xf * lax.rsqrt(jnp.mean(xf * xf, axis=-1, keepdims=True) + EPS)
    return (y * g.astype(jnp.float32)).astype(x.dtype)


def gated_group_rmsnorm(y, z, g):
    v = (y * jax.nn.silu(z)).astype(jnp.float32)
    shp = v.shape
    v = v.reshape(shp[:-1] + (SSD_GROUPS, shp[-1] // SSD_GROUPS))
    v = v * lax.rsqrt(jnp.mean(v * v, axis=-1, keepdims=True) + EPS)
    return (v.reshape(shp) * g.astype(jnp.float32)).astype(y.dtype)


def layernorm(x, g, b):
    xf = x.astype(jnp.float32)
    mu = jnp.mean(xf, axis=-1, keepdims=True)
    xc = xf - mu
    y = xc * lax.rsqrt(jnp.mean(xc * xc, axis=-1, keepdims=True) + EPS)
    return (y * g.astype(jnp.float32) + b.astype(jnp.float32)).astype(x.dtype)


def causal_depthwise_conv(x, w, b):
    k = w.shape[0]
    y = lax.conv_general_dilated(
        x, w[:, None, :].astype(x.dtype), window_strides=(1,), padding=[(k - 1, 0)],
        dimension_numbers=('NWC', 'WIO', 'NWC'), feature_group_count=x.shape[-1])
    return y + b.astype(x.dtype)


def ssd_chunked(x, dt, A, Bm, Cm):
    b, l, h, p = x.shape
    g, n = Bm.shape[-2:]
    e = h // g
    c = l // CHUNK
    dtype = x.dtype
    Xc = (x * dt[..., None].astype(dtype)).reshape(b, c, CHUNK, g, e, p)
    Bc = Bm.reshape(b, c, CHUNK, g, n)
    Cc = Cm.reshape(b, c, CHUNK, g, n)
    a = jnp.transpose((dt * A).reshape(b, c, CHUNK, g, e), (0, 3, 4, 1, 2))
    a_cs = jnp.cumsum(a, axis=-1)
    causal = jnp.tril(jnp.ones((CHUNK, CHUNK), dtype=bool))
    seg = a_cs[..., :, None] - a_cs[..., None, :]
    Lmat = jnp.exp(jnp.where(causal, seg, -jnp.inf)).astype(dtype)
    scores = jnp.einsum('bclgn,bcsgn->bgcls', Cc, Bc)
    y_diag = jnp.einsum('bgecls,bcsgep->bclgep', scores[:, :, None] * Lmat, Xc)
    decay_to_end = jnp.exp(a_cs[..., -1:] - a_cs).astype(dtype)
    chunk_states = jnp.einsum('bclgn,bgecl,bclgep->cbgepn', Bc, decay_to_end, Xc)
    chunk_decay = jnp.moveaxis(jnp.exp(a_cs[..., -1]).astype(dtype), -1, 0)

    def step(state, inp):
        dec, new = inp
        return state * dec[..., None, None] + new, state

    init = jnp.zeros(chunk_states.shape[1:], dtype)
    _, prev_states = lax.scan(step, init, (chunk_decay, chunk_states))
    decay_from_start = jnp.exp(a_cs).astype(dtype)
    y_off = jnp.einsum('bclgn,cbgepn,bgecl->bclgep', Cc, prev_states, decay_from_start)
    return (y_diag + y_off).reshape(b, l, h, p)


def setup_inputs(seed: int = 0) -> dict:
    key = jax.random.key(seed)
    ks = jax.random.split(key, 32)
    L = DEPTH
    f32 = jnp.float32

    def nrm(k, shape, scale):
        return jax.random.normal(k, shape, f32) * scale

    def gain(k, shape):
        return 1.0 + 0.01 * jax.random.normal(k, shape, f32)

    dt0 = jnp.exp(jax.random.uniform(ks[6], (L, SSD_HEADS), f32,
                                     minval=math.log(1e-3), maxval=math.log(1e-1)))
    dt_bias = dt0 + jnp.log(-jnp.expm1(-dt0))
    return {
        "x": nrm(ks[0], (BATCH, SEQ, D_MODEL), 1.0),
        "p": nrm(ks[1], (DEPTH, BATCH, SEQ, PLE_DIM), 1.0),
        "mix_norm_g": gain(ks[2], (L, D_MODEL)),
        "w_in": nrm(ks[3], (L, D_MODEL, IN_WIDTH), D_MODEL ** -0.5),
        "ssd_conv_w": nrm(ks[4], (L, SSD_CONV, XBC_WIDTH), SSD_CONV ** -0.5),
        "ssd_conv_b": nrm(ks[5], (L, XBC_WIDTH), 0.01),
        "dt_bias": dt_bias,
        "A_log": jnp.log(jax.random.uniform(ks[7], (L, SSD_HEADS), f32, minval=1.0, maxval=16.0)),
        "D_skip": gain(ks[8], (L, SSD_HEADS)),
        "ssd_norm_g": gain(ks[9], (L, SSD_WIDTH)),
        "conf_dw_w": nrm(ks[10], (L, CONF_KERNEL, CONF_WIDTH), CONF_KERNEL ** -0.5),
        "conf_dw_b": nrm(ks[11], (L, CONF_WIDTH), 0.01),
        "conf_ln_g": gain(ks[12], (L, CONF_WIDTH)),
        "conf_ln_b": nrm(ks[13], (L, CONF_WIDTH), 0.01),
        "w_out": nrm(ks[14], (L, MIX_WIDTH, D_MODEL), MIX_WIDTH ** -0.5),
        "mlp_norm_g": gain(ks[15], (L, D_MODEL)),
        "w_up": nrm(ks[16], (L, D_MODEL, D_FF), D_MODEL ** -0.5),
        "w_down": nrm(ks[17], (L, D_FF, D_MODEL), D_FF ** -0.5),
        "ple_gate_norm_g": gain(ks[18], (L, D_MODEL)),
        "w_ple_gate": nrm(ks[19], (L, D_MODEL, D_MODEL), D_MODEL ** -0.5),
        "b_ple_gate": nrm(ks[20], (L, D_MODEL), 0.01),
        "w_ple": nrm(ks[21], (L, PLE_DIM, D_MODEL), PLE_DIM ** -0.5),
        "ple_norm_g": gain(ks[22], (L, D_MODEL)),
        "final_norm_g": gain(ks[23], (D_MODEL,)),
    }


def reference(x, p, mix_norm_g, w_in, ssd_conv_w, ssd_conv_b, dt_bias, A_log, D_skip,
              ssd_norm_g, conf_dw_w, conf_dw_b, conf_ln_g, conf_ln_b, w_out, mlp_norm_g,
              w_up, w_down, ple_gate_norm_g, w_ple_gate, b_ple_gate, w_ple, ple_norm_g,
              final_norm_g):
    b, l, _ = x.shape
    split_at = np.cumsum([SSD_WIDTH, XBC_WIDTH, SSD_HEADS, CONF_WIDTH]).tolist()
    xbc_split = [SSD_WIDTH, SSD_WIDTH + SSD_GROUPS * SSD_STATE]
    h = x
    for i in range(DEPTH):
        u = rmsnorm(h, mix_norm_g[i])
        proj = u @ w_in[i].astype(u.dtype)
        z, xbc, dt_raw, conf_val, conf_gate = jnp.split(proj, split_at, axis=-1)

        xbc = jax.nn.silu(causal_depthwise_conv(xbc, ssd_conv_w[i], ssd_conv_b[i]))
        xs, Bm, Cm = jnp.split(xbc, xbc_split, axis=-1)
        dt = jax.nn.softplus(dt_raw.astype(jnp.float32) + dt_bias[i].astype(jnp.float32))
        A = -jnp.exp(A_log[i].astype(jnp.float32))
        xh = xs.reshape(b, l, SSD_HEADS, SSD_HEAD_DIM)
        y = ssd_chunked(xh, dt, A,
                        Bm.reshape(b, l, SSD_GROUPS, SSD_STATE),
                        Cm.reshape(b, l, SSD_GROUPS, SSD_STATE))
        y = (y + xh * D_skip[i].astype(xh.dtype)[:, None]).reshape(b, l, SSD_WIDTH)
        y_ssd = gated_group_rmsnorm(y, z, ssd_norm_g[i])

        v = conf_val * jax.nn.sigmoid(conf_gate)
        v = causal_depthwise_conv(v, conf_dw_w[i], conf_dw_b[i])
        y_conf = jax.nn.silu(layernorm(v, conf_ln_g[i], conf_ln_b[i]))

        mixed = jnp.concatenate([y_ssd, y_conf], axis=-1)
        h = h + mixed @ w_out[i].astype(mixed.dtype)

        u = rmsnorm(h, mlp_norm_g[i])
        hid = jax.nn.relu(u @ w_up[i].astype(u.dtype))
        h = h + (hid * hid) @ w_down[i].astype(hid.dtype)

        gate = jax.nn.sigmoid(rmsnorm(h, ple_gate_norm_g[i]) @ w_ple_gate[i].astype(h.dtype)
                              + b_ple_gate[i].astype(h.dtype))
        emb = rmsnorm(p[i].astype(h.dtype) @ w_ple[i].astype(h.dtype), ple_norm_g[i])
        h = h + gate * emb
    return rmsnorm(h, final_norm_g)
```

```python
import functools

import numpy as np
import jax
import jax.numpy as jnp
from jax import lax
from jax.experimental import pallas as pl
from jax.experimental.pallas import tpu as pltpu

D_MODEL = 1024
SSD_WIDTH = 1024
SSD_HEAD_DIM = 64
SSD_HEADS = 16
SSD_GROUPS = 2
SSD_STATE = 128
SSD_CONV = 4
CHUNK = 128
CONF_WIDTH = 1024
CONF_KERNEL = 31
D_FF = 4096
PLE_DIM = 256
EPS = 1e-6
XBC_WIDTH = SSD_WIDTH + 2 * SSD_GROUPS * SSD_STATE
GROUP_WIDTH = SSD_WIDTH // SSD_GROUPS
HEADS_PER_GROUP = SSD_HEADS // SSD_GROUPS

LANES = 128
SUBLANES = 8
DT_COPIES = 3
SSD_HALO = SUBLANES
CONF_HALO = 32
CONV_ROWS = 32

TL = 256
TM = 512
VMEM_LIMIT = 56 * 1024 * 1024

F32 = jnp.float32
BF16 = jnp.bfloat16


def _sigmoid(x):
    return 1.0 / (1.0 + jnp.exp(-x))


def _silu(x):
    return x * _sigmoid(x)


def _softplus(x):
    return jnp.maximum(x, 0.0) + jnp.log1p(jnp.exp(-jnp.abs(x)))


def _rmsnorm(x, g):
    return x * lax.rsqrt(jnp.mean(x * x, axis=-1, keepdims=True) + EPS) * g


def _dot(a, b):
    return jnp.dot(a, b, preferred_element_type=F32)


def _split3_lanes(v, lane):
    hi = v.astype(BF16).astype(F32)
    r1 = v - hi
    mid = r1.astype(BF16).astype(F32)
    lo = r1 - mid
    return jnp.where(lane < SSD_HEADS, hi, jnp.where(lane < 2 * SSD_HEADS, mid, lo)).astype(BF16)


def _cumsum_rows(a, row):
    s = 1
    while s < a.shape[0]:
        a = a + jnp.where(row >= s, pltpu.roll(a, s, axis=0), 0.0)
        s *= 2
    return a


def _mixer_kernel(x_ref, g_ref, wz_ref, wxbc_ref, wdt_ref, wcv_ref, wcg_ref,
                  scw_ref, scb_ref, dtb_ref, aneg_ref, dskip_ref, sng_ref,
                  cw_ref, cb_ref, lng_ref, lnb_ref, wout_ref, e128_ref, e64_ref,
                  out_ref, xbuf, vbuf, cbuf, ybuf, state):
    t = pl.program_id(1)

    @pl.when(t == 0)
    def _():
        xbuf[0:SSD_HALO, :] = jnp.zeros((SSD_HALO, XBC_WIDTH), F32)
        vbuf[0:CONF_HALO, :] = jnp.zeros((CONF_HALO, CONF_WIDTH), F32)
        state[...] = jnp.zeros(state.shape, F32)

    x = x_ref[...]
    ub = _rmsnorm(x, g_ref[...]).astype(BF16)

    xbuf[SSD_HALO:SSD_HALO + TL, :] = _dot(ub, wxbc_ref[...])
    conv = scb_ref[...] + sum(
        scw_ref[k:k + 1, :] * xbuf[pl.ds(SSD_HALO - (SSD_CONV - 1) + k, TL), :]
        for k in range(SSD_CONV))
    xbuf[0:SSD_HALO, :] = xbuf[TL:TL + SSD_HALO, :]
    xbc = _silu(conv)

    dt = _softplus(_dot(ub, wdt_ref[...]) + dtb_ref[...])
    a = dt * aneg_ref[...]

    lane = lax.broadcasted_iota(jnp.int32, (CHUNK, LANES), 1)
    row = lax.broadcasted_iota(jnp.int32, (CHUNK, LANES), 0)
    causal = row >= lane
    low_half = lane < SSD_HEAD_DIM

    for c in range(TL // CHUNK):
        r0 = c * CHUNK
        xs = xbc[r0:r0 + CHUNK, :SSD_WIDTH]
        xs_bf = xs.astype(BF16)
        b_bf = xbc[r0:r0 + CHUNK, SSD_WIDTH:SSD_WIDTH + SSD_GROUPS * SSD_STATE].astype(BF16)
        c_bf = xbc[r0:r0 + CHUNK, SSD_WIDTH + SSD_GROUPS * SSD_STATE:].astype(BF16)
        dt_c = dt[r0:r0 + CHUNK, :]
        a_cs = _cumsum_rows(a[r0:r0 + CHUNK, :], row)
        a_last = a_cs[CHUNK - 1:CHUNK, :]

        colb = _dot(_split3_lanes(a_cs, lane), e128_ref[...])
        w64 = _dot(_split3_lanes(dt_c * jnp.exp(a_last - a_cs), lane), e64_ref[...])
        acs_t = a_cs.T
        dt_t = dt_c.T

        e64_blocks = []
        y_blocks = []
        for g in range(SSD_GROUPS):
            bg = b_bf[:, g * SSD_STATE:(g + 1) * SSD_STATE]
            cg = c_bf[:, g * SSD_STATE:(g + 1) * SSD_STATE]
            scores = lax.dot_general(cg, bg, (((1,), (1,)), ((), ())),
                                     preferred_element_type=F32)
            for jp in range(HEADS_PER_GROUP // 2):
                j = g * (HEADS_PER_GROUP // 2) + jp
                ms = []
                for h in (2 * j, 2 * j + 1):
                    seg = colb[:, h * LANES:(h + 1) * LANES] - acs_t[h:h + 1, :]
                    lmat = jnp.where(causal, jnp.exp(seg), 0.0)
                    ms.append((scores * lmat * dt_t[h:h + 1, :]).astype(BF16))
                xp = xs_bf[:, j * LANES:(j + 1) * LANES]
                zero = jnp.zeros_like(xp)
                rhs = jnp.concatenate([jnp.where(low_half, xp, zero),
                                       jnp.where(low_half, zero, xp)], axis=0)
                y_blocks.append(_dot(jnp.concatenate(ms, axis=1), rhs))
                e64_blocks.append(jnp.exp(jnp.where(
                    low_half, colb[:, (2 * j) * LANES:(2 * j + 1) * LANES],
                    colb[:, (2 * j + 1) * LANES:(2 * j + 2) * LANES])))
        e64 = jnp.concatenate(e64_blocks, axis=1)
        y_diag = jnp.concatenate(y_blocks, axis=1)

        xd_bf = (xs * w64).astype(BF16)
        y_off = []
        for g in range(SSD_GROUPS):
            gs = slice(g * GROUP_WIDTH, (g + 1) * GROUP_WIDTH)
            bg = b_bf[:, g * SSD_STATE:(g + 1) * SSD_STATE]
            cg = c_bf[:, g * SSD_STATE:(g + 1) * SSD_STATE]
            s_old = state[g]
            y_off.append(_dot(cg, s_old.astype(BF16)) * e64[:, gs])
            new = lax.dot_general(bg, xd_bf[:, gs], (((0,), (0,)), ((), ())),
                                  preferred_element_type=F32)
            state[g] = s_old * e64[CHUNK - 1:CHUNK, gs] + new
        ybuf[r0:r0 + CHUNK, :] = y_diag + jnp.concatenate(y_off, axis=1) + xs * dskip_ref[...]

    z = _dot(ub, wz_ref[...])
    v = ybuf[...] * _silu(z)
    parts = []
    for g in range(SSD_GROUPS):
        vg = v[:, g * GROUP_WIDTH:(g + 1) * GROUP_WIDTH]
        parts.append(vg * lax.rsqrt(jnp.mean(vg * vg, axis=-1, keepdims=True) + EPS))
    y_ssd = (jnp.concatenate(parts, axis=1) * sng_ref[...]).astype(BF16)

    vbuf[CONF_HALO:CONF_HALO + TL, :] = _dot(ub, wcv_ref[...]) * _sigmoid(_dot(ub, wcg_ref[...]))
    first = CONF_HALO - (CONF_KERNEL - 1)
    def conv_cols(i, carry):
        cs = pl.ds(pl.multiple_of(i * LANES, LANES), LANES)
        taps = [jnp.broadcast_to(cw_ref[k:k + 1, cs], (SUBLANES, LANES)) for k in range(CONF_KERNEL)]
        bias = jnp.broadcast_to(cb_ref[0:1, cs], (SUBLANES, LANES))
        for r in range(0, TL, CONV_ROWS):
            acc = jnp.broadcast_to(bias[None], (CONV_ROWS // SUBLANES, SUBLANES, LANES))
            for k in range(CONF_KERNEL):
                win = vbuf[pl.ds(r + first + k, CONV_ROWS), cs]
                acc = acc + win.reshape(CONV_ROWS // SUBLANES, SUBLANES, LANES) * taps[k][None]
            cbuf[pl.ds(r, CONV_ROWS), cs] = acc.reshape(CONV_ROWS, LANES)
        return carry

    lax.fori_loop(0, CONF_WIDTH // LANES, conv_cols, 0)
    vbuf[0:CONF_HALO, :] = vbuf[TL:TL + CONF_HALO, :]

    cv = cbuf[...]
    mu = jnp.mean(cv, axis=-1, keepdims=True)
    xc = cv - mu
    ln = xc * lax.rsqrt(jnp.mean(xc * xc, axis=-1, keepdims=True) + EPS) * lng_ref[...] + lnb_ref[...]
    y_conf = _silu(ln).astype(BF16)

    out_ref[...] = (x + _dot(y_ssd, wout_ref[0:SSD_WIDTH, :])
                    + _dot(y_conf, wout_ref[SSD_WIDTH:, :]))


def _mlp_kernel(h_ref, p_ref, g1_ref, wup_ref, wdown_ref, g2_ref, wgate_ref, bgate_ref,
                wple_ref, g3_ref, gf_ref, out_ref):
    h = h_ref[...]
    u = _rmsnorm(h, g1_ref[...]).astype(BF16)
    hid = jnp.maximum(_dot(u, wup_ref[...]), 0.0)
    h = h + _dot((hid * hid).astype(BF16), wdown_ref[...])
    u = _rmsnorm(h, g2_ref[...]).astype(BF16)
    gate = _sigmoid(_dot(u, wgate_ref[...]) + bgate_ref[...])
    emb = _rmsnorm(_dot(p_ref[...].astype(BF16), wple_ref[...]), g3_ref[...])
    out_ref[...] = _rmsnorm(h + gate * emb, gf_ref[...])


def _const_spec(shape):
    nd = len(shape)
    return pl.BlockSpec(shape, lambda *_: (0,) * nd, pipeline_mode=pl.Buffered(1))


def _expand_matrix(width):
    e = np.zeros((LANES, SSD_HEADS * width), np.float32)
    for copy in range(DT_COPIES):
        for h in range(SSD_HEADS):
            e[copy * SSD_HEADS + h, h * width:(h + 1) * width] = 1.0
    return jnp.asarray(e, BF16)


def _pad_heads(v):
    v = jnp.tile(v.astype(F32), DT_COPIES)
    return jnp.pad(v, (0, LANES - v.shape[0]))[None, :]


def _mixer(x, mix_norm_g, w_in, ssd_conv_w, ssd_conv_b, dt_bias, A_log, D_skip, ssd_norm_g,
           conf_dw_w, conf_dw_b, conf_ln_g, conf_ln_b, w_out):
    b, l, d = x.shape
    o = np.cumsum([SSD_WIDTH, XBC_WIDTH, SSD_HEADS, CONF_WIDTH]).tolist()
    wb = w_in.astype(BF16)
    w_z, w_xbc, w_dt, w_cv, w_cg = (wb[:, :o[0]], wb[:, o[0]:o[1]], wb[:, o[1]:o[2]],
                                    wb[:, o[2]:o[3]], wb[:, o[3]:])
    w_dt = jnp.pad(jnp.tile(w_dt, (1, DT_COPIES)), ((0, 0), (0, LANES - DT_COPIES * SSD_HEADS)))
    row = lambda v: v.astype(F32)[None, :]
    consts = [
        row(mix_norm_g), w_z, w_xbc, w_dt, w_cv, w_cg,
        ssd_conv_w.astype(F32), row(ssd_conv_b), _pad_heads(dt_bias), _pad_heads(-jnp.exp(A_log)),
        row(jnp.repeat(D_skip, SSD_HEAD_DIM)), row(ssd_norm_g),
        conf_dw_w.astype(F32), row(conf_dw_b), row(conf_ln_g), row(conf_ln_b), w_out.astype(BF16),
        _expand_matrix(LANES), _expand_matrix(SSD_HEAD_DIM),
    ]
    tile = pl.BlockSpec((None, TL, d), lambda bi, ti: (bi, ti, 0))
    return pl.pallas_call(
        _mixer_kernel,
        grid=(b, l // TL),
        in_specs=[tile] + [_const_spec(c.shape) for c in consts],
        out_specs=tile,
        out_shape=jax.ShapeDtypeStruct((b, l, d), F32),
        scratch_shapes=[
            pltpu.VMEM((SSD_HALO + TL, XBC_WIDTH), F32),
            pltpu.VMEM((CONF_HALO + TL, CONF_WIDTH), F32),
            pltpu.VMEM((TL, CONF_WIDTH), F32),
            pltpu.VMEM((TL, SSD_WIDTH), F32),
            pltpu.VMEM((SSD_GROUPS, SSD_STATE, GROUP_WIDTH), F32),
        ],
        compiler_params=pltpu.CompilerParams(
            dimension_semantics=("arbitrary", "arbitrary"), vmem_limit_bytes=VMEM_LIMIT),
        name="mixer",
    )(x, *consts)


def _mlp(h, p, mlp_norm_g, w_up, w_down, ple_gate_norm_g, w_ple_gate, b_ple_gate, w_ple,
         ple_norm_g, final_norm_g):
    n, d = h.shape
    row = lambda v: v.astype(F32)[None, :]
    consts = [row(mlp_norm_g), w_up.astype(BF16), w_down.astype(BF16), row(ple_gate_norm_g),
              w_ple_gate.astype(BF16), row(b_ple_gate), w_ple.astype(BF16), row(ple_norm_g),
              row(final_norm_g)]
    return pl.pallas_call(
        _mlp_kernel,
        grid=(n // TM,),
        in_specs=[pl.BlockSpec((TM, d), lambda i: (i, 0)),
                  pl.BlockSpec((TM, PLE_DIM), lambda i: (i, 0))]
                 + [_const_spec(c.shape) for c in consts],
        out_specs=pl.BlockSpec((TM, d), lambda i: (i, 0)),
        out_shape=jax.ShapeDtypeStruct((n, d), F32),
        compiler_params=pltpu.CompilerParams(
            dimension_semantics=("arbitrary",), vmem_limit_bytes=VMEM_LIMIT),
        name="mlp",
    )(h, p, *consts)


def kernel(x, p, mix_norm_g, w_in, ssd_conv_w, ssd_conv_b, dt_bias, A_log, D_skip, ssd_norm_g,
           conf_dw_w, conf_dw_b, conf_ln_g, conf_ln_b, w_out, mlp_norm_g, w_up, w_down,
           ple_gate_norm_g, w_ple_gate, b_ple_gate, w_ple, ple_norm_g, final_norm_g):
    b, l, d = x.shape
    assert (d, l % TL, (b * l) % TM, mix_norm_g.shape[0]) == (D_MODEL, 0, 0, 1)
    h = _mixer(x, mix_norm_g[0], w_in[0], ssd_conv_w[0], ssd_conv_b[0], dt_bias[0], A_log[0],
               D_skip[0], ssd_norm_g[0], conf_dw_w[0], conf_dw_b[0], conf_ln_g[0], conf_ln_b[0],
               w_out[0])
    out = _mlp(h.reshape(b * l, d), p[0].reshape(b * l, PLE_DIM), mlp_norm_g[0], w_up[0],
               w_down[0], ple_gate_norm_g[0], w_ple_gate[0], b_ple_gate[0], w_ple[0],
               ple_norm_g[0], final_norm_g)
    return out.reshape(b, l, d)
```

```python
import math

import numpy as np
import jax
import jax.numpy as jnp
from jax import lax
from jax.experimental import pallas as pl
from jax.experimental.pallas import tpu as pltpu

D_MODEL = 1024
SSD_WIDTH = 1024
SSD_HEAD_DIM = 64
SSD_HEADS = 16
SSD_GROUPS = 2
SSD_STATE = 128
SSD_CONV = 4
CHUNK = 128
CONF_WIDTH = 1024
CONF_KERNEL = 31
D_FF = 4096
PLE_DIM = 256
EPS = 1e-6
XBC_WIDTH = SSD_WIDTH + 2 * SSD_GROUPS * SSD_STATE
GROUP_WIDTH = SSD_WIDTH // SSD_GROUPS
HEADS_PER_GROUP = SSD_HEADS // SSD_GROUPS

LANES = 128
SUBLANES = 8
SPLIT = 3
SSD_HALO = SUBLANES
CONF_HALO = 32
CONV_ROWS = 32

TL = 256
TM = 512
VMEM_LIMIT = 56 * 1024 * 1024
LOG2E = math.log2(math.e)

F32 = jnp.float32
BF16 = jnp.bfloat16


def _sigmoid(x):
    return 1.0 / (1.0 + jnp.exp(-x))


def _silu(x):
    return x * _sigmoid(x)


def _softplus(x):
    return jnp.maximum(x, 0.0) + jnp.log1p(jnp.exp(-jnp.abs(x)))


def _rmsnorm(x, g):
    return x * lax.rsqrt(jnp.mean(x * x, axis=-1, keepdims=True) + EPS) * g


def _dot(a, b):
    return jnp.dot(a, b, preferred_element_type=F32)


def _split3_cols(v_t):
    hi = v_t.astype(BF16).astype(F32)
    r1 = v_t - hi
    mid = r1.astype(BF16).astype(F32)
    lo = r1 - mid
    pad = jnp.zeros((LANES - SPLIT * SSD_HEADS, v_t.shape[1]), F32)
    return jnp.concatenate([hi, mid, lo, pad], axis=0).T.astype(BF16)


def _cumsum_lanes(a, lane):
    s = 1
    while s < a.shape[1]:
        a = a + jnp.where(lane >= s, pltpu.roll(a, s, axis=1), 0.0)
        s *= 2
    return a


def _conv31(vbuf, shbuf, cw_ref, cb_ref, cbuf, rows):
    first = CONF_HALO - (CONF_KERNEL - 1)
    span = rows + CONF_HALO - SUBLANES
    for r in range(1, SUBLANES):
        shbuf[r - 1] = vbuf[pl.ds(r, span), :]
    for cbk in range(CONF_WIDTH // LANES):
        cs = slice(cbk * LANES, (cbk + 1) * LANES)
        for rb in range(0, rows, CONV_ROWS):
            acc = jnp.broadcast_to(cb_ref[0:1, cs], (CONV_ROWS, LANES))
            for k in range(CONF_KERNEL):
                q, r = divmod(first + k, SUBLANES)
                lo = q * SUBLANES + rb
                src = shbuf.at[r - 1] if r else vbuf
                acc = acc + src[pl.ds(lo, CONV_ROWS), cs] * cw_ref[k:k + 1, cs]
            cbuf[rb:rb + CONV_ROWS, cs] = acc


def _mixer_kernel(x_ref, g_ref, wz_ref, wxbc_ref, wdt_ref, wcv_ref, wcg_ref,
                  scw_ref, scb_ref, dtb_ref, aneg_ref, dskip_ref, sng_ref,
                  cw_ref, cb_ref, lng_ref, lnb_ref, wout_ref, e128_ref, e64_ref,
                  out_ref, xbuf, vbuf, shbuf, cbuf, ybuf, state):
    t = pl.program_id(1)

    @pl.when(t == 0)
    def _():
        xbuf[0:SSD_HALO, :] = jnp.zeros((SSD_HALO, XBC_WIDTH), F32)
        vbuf[0:CONF_HALO, :] = jnp.zeros((CONF_HALO, CONF_WIDTH), F32)
        state[...] = jnp.zeros(state.shape, F32)

    x = x_ref[...]
    ub = _rmsnorm(x, g_ref[...]).astype(BF16)

    xbuf[SSD_HALO:SSD_HALO + TL, :] = _dot(ub, wxbc_ref[...])
    conv = scb_ref[...] + sum(
        scw_ref[k:k + 1, :] * xbuf[pl.ds(SSD_HALO - (SSD_CONV - 1) + k, TL), :]
        for k in range(SSD_CONV))
    xbuf[0:SSD_HALO, :] = xbuf[TL:TL + SSD_HALO, :]
    xbc = _silu(conv)

    dt = _softplus(_dot(ub, wdt_ref[...]) + dtb_ref[...])

    lane = lax.broadcasted_iota(jnp.int32, (CHUNK, LANES), 1)
    row = lax.broadcasted_iota(jnp.int32, (CHUNK, LANES), 0)
    causal = row >= lane
    low_half = lane < SSD_HEAD_DIM
    lane_h = lax.broadcasted_iota(jnp.int32, (SSD_HEADS, CHUNK), 1)

    for c in range(TL // CHUNK):
        r0 = c * CHUNK
        xs = xbc[r0:r0 + CHUNK, :SSD_WIDTH]
        xs_bf = xs.astype(BF16)
        b_bf = xbc[r0:r0 + CHUNK, SSD_WIDTH:SSD_WIDTH + SSD_GROUPS * SSD_STATE].astype(BF16)
        c_bf = xbc[r0:r0 + CHUNK, SSD_WIDTH + SSD_GROUPS * SSD_STATE:].astype(BF16)

        dt_t = dt[r0:r0 + CHUNK, :].T[0:SSD_HEADS]
        acs_t = _cumsum_lanes(dt_t * aneg_ref[...], lane_h)
        acs2_t = acs_t * LOG2E
        rows_t = acs2_t - jnp.log2(dt_t)
        w_t = dt_t * jnp.exp(acs_t[:, CHUNK - 1:CHUNK] - acs_t)

        colb = _dot(_split3_cols(acs2_t), e128_ref[...])
        w64 = _dot(_split3_cols(w_t), e64_ref[...])

        e64_blocks = []
        y_blocks = []
        for g in range(SSD_GROUPS):
            bg = b_bf[:, g * SSD_STATE:(g + 1) * SSD_STATE]
            cg = c_bf[:, g * SSD_STATE:(g + 1) * SSD_STATE]
            scores = lax.dot_general(cg, bg, (((1,), (1,)), ((), ())),
                                     preferred_element_type=F32).astype(BF16)
            for jp in range(HEADS_PER_GROUP // 2):
                j = g * (HEADS_PER_GROUP // 2) + jp
                ms = []
                for h in (2 * j, 2 * j + 1):
                    seg = colb[:, h * LANES:(h + 1) * LANES] - rows_t[h:h + 1, :]
                    lmat = jnp.where(causal, jnp.exp2(seg), 0.0).astype(BF16)
                    ms.append(lmat * scores)
                xp = xs_bf[:, j * LANES:(j + 1) * LANES]
                zero = jnp.zeros_like(xp)
                rhs = jnp.concatenate([jnp.where(low_half, xp, zero),
                                       jnp.where(low_half, zero, xp)], axis=0)
                y_blocks.append(_dot(jnp.concatenate(ms, axis=1), rhs))
                e64_blocks.append(jnp.exp2(jnp.where(
                    low_half, colb[:, (2 * j) * LANES:(2 * j + 1) * LANES],
                    colb[:, (2 * j + 1) * LANES:(2 * j + 2) * LANES])))
        e64 = jnp.concatenate(e64_blocks, axis=1)
        y_diag = jnp.concatenate(y_blocks, axis=1)

        xd_bf = (xs * w64).astype(BF16)
        y_off = []
        for g in range(SSD_GROUPS):
            gs = slice(g * GROUP_WIDTH, (g + 1) * GROUP_WIDTH)
            bg = b_bf[:, g * SSD_STATE:(g + 1) * SSD_STATE]
            cg = c_bf[:, g * SSD_STATE:(g + 1) * SSD_STATE]
            s_old = state[g]
            y_off.append(_dot(cg, s_old.astype(BF16)) * e64[:, gs])
            new = lax.dot_general(bg, xd_bf[:, gs], (((0,), (0,)), ((), ())),
                                  preferred_element_type=F32)
            state[g] = s_old * e64[CHUNK - 1:CHUNK, gs] + new
        ybuf[r0:r0 + CHUNK, :] = y_diag + jnp.concatenate(y_off, axis=1) + xs * dskip_ref[...]

    z = _dot(ub, wz_ref[...])
    v = ybuf[...] * _silu(z)
    parts = []
    for g in range(SSD_GROUPS):
        vg = v[:, g * GROUP_WIDTH:(g + 1) * GROUP_WIDTH]
        parts.append(vg * lax.rsqrt(jnp.mean(vg * vg, axis=-1, keepdims=True) + EPS))
    y_ssd = (jnp.concatenate(parts, axis=1) * sng_ref[...]).astype(BF16)

    vbuf[CONF_HALO:CONF_HALO + TL, :] = _dot(ub, wcv_ref[...]) * _sigmoid(_dot(ub, wcg_ref[...]))
    _conv31(vbuf, shbuf, cw_ref, cb_ref, cbuf, TL)
    vbuf[0:CONF_HALO, :] = vbuf[TL:TL + CONF_HALO, :]

    cv = cbuf[...]
    mu = jnp.mean(cv, axis=-1, keepdims=True)
    xc = cv - mu
    ln = xc * lax.rsqrt(jnp.mean(xc * xc, axis=-1, keepdims=True) + EPS) * lng_ref[...] + lnb_ref[...]
    y_conf = _silu(ln).astype(BF16)

    out_ref[...] = (x + _dot(y_ssd, wout_ref[0:SSD_WIDTH, :])
                    + _dot(y_conf, wout_ref[SSD_WIDTH:, :]))


def _mlp_kernel(h_ref, p_ref, g1_ref, wup_ref, wdown_ref, g2_ref, wgate_ref, bgate_ref,
                wple_ref, g3_ref, gf_ref, out_ref):
    h = h_ref[...]
    u = _rmsnorm(h, g1_ref[...]).astype(BF16)
    hid = jnp.maximum(_dot(u, wup_ref[...]), 0.0)
    h = h + _dot((hid * hid).astype(BF16), wdown_ref[...])
    u = _rmsnorm(h, g2_ref[...]).astype(BF16)
    gate = _sigmoid(_dot(u, wgate_ref[...]) + bgate_ref[...])
    emb = _rmsnorm(_dot(p_ref[...].astype(BF16), wple_ref[...]), g3_ref[...])
    out_ref[...] = _rmsnorm(h + gate * emb, gf_ref[...])


def _const_spec(shape):
    nd = len(shape)
    return pl.BlockSpec(shape, lambda *_: (0,) * nd, pipeline_mode=pl.Buffered(1))


def _expand_matrix(width):
    e = np.zeros((LANES, SSD_HEADS * width), np.float32)
    for piece in range(SPLIT):
        for h in range(SSD_HEADS):
            e[piece * SSD_HEADS + h, h * width:(h + 1) * width] = 1.0
    return jnp.asarray(e, BF16)


def _mixer(x, mix_norm_g, w_in, ssd_conv_w, ssd_conv_b, dt_bias, A_log, D_skip, ssd_norm_g,
           conf_dw_w, conf_dw_b, conf_ln_g, conf_ln_b, w_out):
    b, l, d = x.shape
    o = np.cumsum([SSD_WIDTH, XBC_WIDTH, SSD_HEADS, CONF_WIDTH]).tolist()
    wb = w_in.astype(BF16)
    w_z, w_xbc, w_dt, w_cv, w_cg = (wb[:, :o[0]], wb[:, o[0]:o[1]], wb[:, o[1]:o[2]],
                                    wb[:, o[2]:o[3]], wb[:, o[3]:])
    pad_heads = lambda v: jnp.pad(v, ((0, 0), (0, LANES - SSD_HEADS)))
    row = lambda v: v.astype(F32)[None, :]
    consts = [
        row(mix_norm_g), w_z, w_xbc, pad_heads(w_dt), w_cv, w_cg,
        ssd_conv_w.astype(F32), row(ssd_conv_b), pad_heads(row(dt_bias)),
        -jnp.exp(A_log.astype(F32))[:, None],
        row(jnp.repeat(D_skip, SSD_HEAD_DIM)), row(ssd_norm_g),
        conf_dw_w.astype(F32), row(conf_dw_b), row(conf_ln_g), row(conf_ln_b), w_out.astype(BF16),
        _expand_matrix(LANES), _expand_matrix(SSD_HEAD_DIM),
    ]
    tile = pl.BlockSpec((None, TL, d), lambda bi, ti: (bi, ti, 0))
    return pl.pallas_call(
        _mixer_kernel,
        grid=(b, l // TL),
        in_specs=[tile] + [_const_spec(c.shape) for c in consts],
        out_specs=tile,
        out_shape=jax.ShapeDtypeStruct((b, l, d), F32),
        scratch_shapes=[
            pltpu.VMEM((SSD_HALO + TL, XBC_WIDTH), F32),
            pltpu.VMEM((CONF_HALO + TL, CONF_WIDTH), F32),
            pltpu.VMEM((SUBLANES - 1, TL + CONF_HALO - SUBLANES, CONF_WIDTH), F32),
            pltpu.VMEM((TL, CONF_WIDTH), F32),
            pltpu.VMEM((TL, SSD_WIDTH), F32),
            pltpu.VMEM((SSD_GROUPS, SSD_STATE, GROUP_WIDTH), F32),
        ],
        compiler_params=pltpu.CompilerParams(
            dimension_semantics=("arbitrary", "arbitrary"), vmem_limit_bytes=VMEM_LIMIT),
        name="mixer",
    )(x, *consts)


def _mlp(h, p, mlp_norm_g, w_up, w_down, ple_gate_norm_g, w_ple_gate, b_ple_gate, w_ple,
         ple_norm_g, final_norm_g):
    n, d = h.shape
    row = lambda v: v.astype(F32)[None, :]
    consts = [row(mlp_norm_g), w_up.astype(BF16), w_down.astype(BF16), row(ple_gate_norm_g),
              w_ple_gate.astype(BF16), row(b_ple_gate), w_ple.astype(BF16), row(ple_norm_g),
              row(final_norm_g)]
    return pl.pallas_call(
        _mlp_kernel,
        grid=(n // TM,),
        in_specs=[pl.BlockSpec((TM, d), lambda i: (i, 0)),
                  pl.BlockSpec((TM, PLE_DIM), lambda i: (i, 0))]
                 + [_const_spec(c.shape) for c in consts],
        out_specs=pl.BlockSpec((TM, d), lambda i: (i, 0)),
        out_shape=jax.ShapeDtypeStruct((n, d), F32),
        compiler_params=pltpu.CompilerParams(
            dimension_semantics=("arbitrary",), vmem_limit_bytes=VMEM_LIMIT),
        name="mlp",
    )(h, p, *consts)


def kernel(x, p, mix_norm_g, w_in, ssd_conv_w, ssd_conv_b, dt_bias, A_log, D_skip, ssd_norm_g,
           conf_dw_w, conf_dw_b, conf_ln_g, conf_ln_b, w_out, mlp_norm_g, w_up, w_down,
           ple_gate_norm_g, w_ple_gate, b_ple_gate, w_ple, ple_norm_g, final_norm_g):
    b, l, d = x.shape
    assert (d, l % TL, (b * l) % TM, mix_norm_g.shape[0]) == (D_MODEL, 0, 0, 1)
    h = _mixer(x, mix_norm_g[0], w_in[0], ssd_conv_w[0], ssd_conv_b[0], dt_bias[0], A_log[0],
               D_skip[0], ssd_norm_g[0], conf_dw_w[0], conf_dw_b[0], conf_ln_g[0], conf_ln_b[0],
               w_out[0])
    out = _mlp(h.reshape(b * l, d), p[0].reshape(b * l, PLE_DIM), mlp_norm_g[0], w_up[0],
               w_down[0], ple_gate_norm_g[0], w_ple_gate[0], b_ple_gate[0], w_ple[0],
               ple_norm_g[0], final_norm_g)
    return out.reshape(b, l, d)
```

```python
import math

import numpy as np
import jax
import jax.numpy as jnp
from jax import lax
from jax.experimental import pallas as pl
from jax.experimental.pallas import tpu as pltpu

D_MODEL = 1024
SSD_WIDTH = 1024
SSD_HEAD_DIM = 64
SSD_HEADS = 16
SSD_GROUPS = 2
SSD_STATE = 128
SSD_CONV = 4
CHUNK = 128
CONF_WIDTH = 1024
CONF_KERNEL = 31
D_FF = 4096
PLE_DIM = 256
EPS = 1e-6
XBC_WIDTH = SSD_WIDTH + 2 * SSD_GROUPS * SSD_STATE
GROUP_WIDTH = SSD_WIDTH // SSD_GROUPS
HEADS_PER_GROUP = SSD_HEADS // SSD_GROUPS

LANES = 128
SUBLANES = 8
SPLIT = 3
SSD_HALO = SUBLANES
CONF_HALO = 32
CONV_ROWS = 32

TL = 512
TM = 512
VMEM_LIMIT = 56 * 1024 * 1024
LOG2E = math.log2(math.e)

F32 = jnp.float32
BF16 = jnp.bfloat16


def _sigmoid(x):
    return 1.0 / (1.0 + jnp.exp(-x))


def _silu(x):
    return x * _sigmoid(x)


def _softplus(x):
    return jnp.maximum(x, 0.0) + jnp.log1p(jnp.exp(-jnp.abs(x)))


def _rmsnorm(x, g):
    return x * lax.rsqrt(jnp.mean(x * x, axis=-1, keepdims=True) + EPS) * g


def _dot(a, b):
    return jnp.dot(a, b, preferred_element_type=F32)


def _split3_cols(v_t):
    hi = v_t.astype(BF16).astype(F32)
    r1 = v_t - hi
    mid = r1.astype(BF16).astype(F32)
    lo = r1 - mid
    pad = jnp.zeros((LANES - SPLIT * SSD_HEADS, v_t.shape[1]), F32)
    return jnp.concatenate([hi, mid, lo, pad], axis=0).T.astype(BF16)


def _cumsum_lanes(a, lane):
    s = 1
    while s < a.shape[1]:
        a = a + jnp.where(lane >= s, pltpu.roll(a, s, axis=1), 0.0)
        s *= 2
    return a


def _conv31(vbuf, shbuf, cw_ref, cb_ref, cbuf, rows):
    first = CONF_HALO - (CONF_KERNEL - 1)
    span = rows + CONF_HALO - SUBLANES
    for cbk in range(CONF_WIDTH // LANES):
        cs = slice(cbk * LANES, (cbk + 1) * LANES)
        sh = shbuf.at[cbk % shbuf.shape[0]]
        for r in range(1, SUBLANES):
            sh[r - 1] = vbuf[pl.ds(r, span), cs]
        for rb in range(0, rows, CONV_ROWS):
            acc = jnp.broadcast_to(cb_ref[0:1, cs], (CONV_ROWS, LANES))
            for k in range(CONF_KERNEL):
                q, r = divmod(first + k, SUBLANES)
                lo = q * SUBLANES + rb
                win = sh[r - 1, pl.ds(lo, CONV_ROWS), :] if r else vbuf[pl.ds(lo, CONV_ROWS), cs]
                acc = acc + win * cw_ref[k:k + 1, cs]
            cbuf[rb:rb + CONV_ROWS, cs] = acc


def _mixer_kernel(x_ref, g_ref, wz_ref, wxbc_ref, wdt_ref, wcv_ref, wcg_ref,
                  scw_ref, scb_ref, dtb_ref, aneg_ref, dskip_ref, sng_ref,
                  cw_ref, cb_ref, lng_ref, lnb_ref, wout_ref, e128_ref, e64_ref,
                  out_ref, xbuf, vbuf, shbuf, cbuf, ybuf, state):
    t = pl.program_id(1)

    @pl.when(t == 0)
    def _():
        xbuf[:, TL:TL + SSD_HALO, :] = jnp.zeros((SSD_CONV - 1, SSD_HALO, XBC_WIDTH), F32)
        vbuf[0:CONF_HALO, :] = jnp.zeros((CONF_HALO, CONF_WIDTH), F32)
        state[...] = jnp.zeros(state.shape, F32)

    x = x_ref[...]
    ub = _rmsnorm(x, g_ref[...]).astype(BF16)

    pre = _dot(ub, wxbc_ref[...])
    conv = scb_ref[...] + scw_ref[SSD_CONV - 1:SSD_CONV, :] * pre
    for k in range(SSD_CONV - 1):
        xbuf[k, 0:SSD_HALO, :] = xbuf[k, TL:TL + SSD_HALO, :]
        xbuf[k, pl.ds(SSD_CONV - 1 - k, TL), :] = pre
        conv = conv + scw_ref[k:k + 1, :] * xbuf[k, 0:TL, :]
    xbc = _silu(conv)

    dt = _softplus(_dot(ub, wdt_ref[...]) + dtb_ref[...])

    lane = lax.broadcasted_iota(jnp.int32, (CHUNK, LANES), 1)
    row = lax.broadcasted_iota(jnp.int32, (CHUNK, LANES), 0)
    causal = row >= lane
    low_half = lane < SSD_HEAD_DIM
    lane_h = lax.broadcasted_iota(jnp.int32, (SSD_HEADS, CHUNK), 1)

    for c in range(TL // CHUNK):
        r0 = c * CHUNK
        xs = xbc[r0:r0 + CHUNK, :SSD_WIDTH]
        xs_bf = xs.astype(BF16)
        b_bf = xbc[r0:r0 + CHUNK, SSD_WIDTH:SSD_WIDTH + SSD_GROUPS * SSD_STATE].astype(BF16)
        c_bf = xbc[r0:r0 + CHUNK, SSD_WIDTH + SSD_GROUPS * SSD_STATE:].astype(BF16)

        dt_t = dt[r0:r0 + CHUNK, :].T[0:SSD_HEADS]
        acs_t = _cumsum_lanes(dt_t * aneg_ref[...], lane_h)
        acs2_t = acs_t * LOG2E
        rows_t = acs2_t - jnp.log2(dt_t)
        w_t = dt_t * jnp.exp(acs_t[:, CHUNK - 1:CHUNK] - acs_t)

        colb = _dot(_split3_cols(acs2_t), e128_ref[...])
        w64 = _dot(_split3_cols(w_t), e64_ref[...])

        e64_blocks = []
        y_blocks = []
        for g in range(SSD_GROUPS):
            bg = b_bf[:, g * SSD_STATE:(g + 1) * SSD_STATE]
            cg = c_bf[:, g * SSD_STATE:(g + 1) * SSD_STATE]
            scores = lax.dot_general(cg, bg, (((1,), (1,)), ((), ())),
                                     preferred_element_type=F32).astype(BF16)
            for jp in range(HEADS_PER_GROUP // 2):
                j = g * (HEADS_PER_GROUP // 2) + jp
                ms = []
                for h in (2 * j, 2 * j + 1):
                    seg = colb[:, h * LANES:(h + 1) * LANES] - rows_t[h:h + 1, :]
                    lmat = jnp.where(causal, jnp.exp2(seg), 0.0).astype(BF16)
                    ms.append(lmat * scores)
                xp = xs_bf[:, j * LANES:(j + 1) * LANES]
                zero = jnp.zeros_like(xp)
                rhs = jnp.concatenate([jnp.where(low_half, xp, zero),
                                       jnp.where(low_half, zero, xp)], axis=0)
                y_blocks.append(_dot(jnp.concatenate(ms, axis=1), rhs))
                e64_blocks.append(jnp.exp2(jnp.where(
                    low_half, colb[:, (2 * j) * LANES:(2 * j + 1) * LANES],
                    colb[:, (2 * j + 1) * LANES:(2 * j + 2) * LANES])))
        e64 = jnp.concatenate(e64_blocks, axis=1)
        y_diag = jnp.concatenate(y_blocks, axis=1)

        xd_bf = (xs * w64).astype(BF16)
        y_off = []
        for g in range(SSD_GROUPS):
            gs = slice(g * GROUP_WIDTH, (g + 1) * GROUP_WIDTH)
            bg = b_bf[:, g * SSD_STATE:(g + 1) * SSD_STATE]
            cg = c_bf[:, g * SSD_STATE:(g + 1) * SSD_STATE]
            s_old = state[g]
            y_off.append(_dot(cg, s_old.astype(BF16)) * e64[:, gs])
            new = lax.dot_general(bg, xd_bf[:, gs], (((0,), (0,)), ((), ())),
                                  preferred_element_type=F32)
            state[g] = s_old * e64[CHUNK - 1:CHUNK, gs] + new
        ybuf[r0:r0 + CHUNK, :] = y_diag + jnp.concatenate(y_off, axis=1) + xs * dskip_ref[...]

    z = _dot(ub, wz_ref[...])
    v = ybuf[...] * _silu(z)
    parts = []
    for g in range(SSD_GROUPS):
        vg = v[:, g * GROUP_WIDTH:(g + 1) * GROUP_WIDTH]
        parts.append(vg * lax.rsqrt(jnp.mean(vg * vg, axis=-1, keepdims=True) + EPS))
    y_ssd = (jnp.concatenate(parts, axis=1) * sng_ref[...]).astype(BF16)

    vbuf[CONF_HALO:CONF_HALO + TL, :] = _dot(ub, wcv_ref[...]) * _sigmoid(_dot(ub, wcg_ref[...]))
    _conv31(vbuf, shbuf, cw_ref, cb_ref, cbuf, TL)
    vbuf[0:CONF_HALO, :] = vbuf[TL:TL + CONF_HALO, :]

    cv = cbuf[...]
    mu = jnp.mean(cv, axis=-1, keepdims=True)
    xc = cv - mu
    ln = xc * lax.rsqrt(jnp.mean(xc * xc, axis=-1, keepdims=True) + EPS) * lng_ref[...] + lnb_ref[...]
    y_conf = _silu(ln).astype(BF16)

    out_ref[...] = (x + _dot(y_ssd, wout_ref[0:SSD_WIDTH, :])
                    + _dot(y_conf, wout_ref[SSD_WIDTH:, :]))


def _mlp_kernel(h_ref, p_ref, g1_ref, wup_ref, wdown_ref, g2_ref, wgate_ref, bgate_ref,
                wple_ref, g3_ref, gf_ref, out_ref):
    h = h_ref[...]
    u = _rmsnorm(h, g1_ref[...]).astype(BF16)
    hid = jnp.maximum(_dot(u, wup_ref[...]), 0.0)
    h = h + _dot((hid * hid).astype(BF16), wdown_ref[...])
    u = _rmsnorm(h, g2_ref[...]).astype(BF16)
    gate = _sigmoid(_dot(u, wgate_ref[...]) + bgate_ref[...])
    emb = _rmsnorm(_dot(p_ref[...].astype(BF16), wple_ref[...]), g3_ref[...])
    out_ref[...] = _rmsnorm(h + gate * emb, gf_ref[...])


def _const_spec(shape):
    nd = len(shape)
    return pl.BlockSpec(shape, lambda *_: (0,) * nd, pipeline_mode=pl.Buffered(1))


def _expand_matrix(width):
    e = np.zeros((LANES, SSD_HEADS * width), np.float32)
    for piece in range(SPLIT):
        for h in range(SSD_HEADS):
            e[piece * SSD_HEADS + h, h * width:(h + 1) * width] = 1.0
    return jnp.asarray(e, BF16)


def _mixer(x, mix_norm_g, w_in, ssd_conv_w, ssd_conv_b, dt_bias, A_log, D_skip, ssd_norm_g,
           conf_dw_w, conf_dw_b, conf_ln_g, conf_ln_b, w_out):
    b, l, d = x.shape
    o = np.cumsum([SSD_WIDTH, XBC_WIDTH, SSD_HEADS, CONF_WIDTH]).tolist()
    wb = w_in.astype(BF16)
    w_z, w_xbc, w_dt, w_cv, w_cg = (wb[:, :o[0]], wb[:, o[0]:o[1]], wb[:, o[1]:o[2]],
                                    wb[:, o[2]:o[3]], wb[:, o[3]:])
    pad_heads = lambda v: jnp.pad(v, ((0, 0), (0, LANES - SSD_HEADS)))
    row = lambda v: v.astype(F32)[None, :]
    consts = [
        row(mix_norm_g), w_z, w_xbc, pad_heads(w_dt), w_cv, w_cg,
        ssd_conv_w.astype(F32), row(ssd_conv_b), pad_heads(row(dt_bias)),
        -jnp.exp(A_log.astype(F32))[:, None],
        row(jnp.repeat(D_skip, SSD_HEAD_DIM)), row(ssd_norm_g),
        conf_dw_w.astype(F32), row(conf_dw_b), row(conf_ln_g), row(conf_ln_b), w_out.astype(BF16),
        _expand_matrix(LANES), _expand_matrix(SSD_HEAD_DIM),
    ]
    tile = pl.BlockSpec((None, TL, d), lambda bi, ti: (bi, ti, 0))
    return pl.pallas_call(
        _mixer_kernel,
        grid=(b, l // TL),
        in_specs=[tile] + [_const_spec(c.shape) for c in consts],
        out_specs=tile,
        out_shape=jax.ShapeDtypeStruct((b, l, d), F32),
        scratch_shapes=[
            pltpu.VMEM((SSD_CONV - 1, TL + SSD_HALO, XBC_WIDTH), F32),
            pltpu.VMEM((CONF_HALO + TL, CONF_WIDTH), F32),
            pltpu.VMEM((2, SUBLANES - 1, TL + CONF_HALO - SUBLANES, LANES), F32),
            pltpu.VMEM((TL, CONF_WIDTH), F32),
            pltpu.VMEM((TL, SSD_WIDTH), F32),
            pltpu.VMEM((SSD_GROUPS, SSD_STATE, GROUP_WIDTH), F32),
        ],
        compiler_params=pltpu.CompilerParams(
            dimension_semantics=("arbitrary", "arbitrary"), vmem_limit_bytes=VMEM_LIMIT),
        name="mixer",
    )(x, *consts)


def _mlp(h, p, mlp_norm_g, w_up, w_down, ple_gate_norm_g, w_ple_gate, b_ple_gate, w_ple,
         ple_norm_g, final_norm_g):
    n, d = h.shape
    row = lambda v: v.astype(F32)[None, :]
    consts = [row(mlp_norm_g), w_up.astype(BF16), w_down.astype(BF16), row(ple_gate_norm_g),
              w_ple_gate.astype(BF16), row(b_ple_gate), w_ple.astype(BF16), row(ple_norm_g),
              row(final_norm_g)]
    return pl.pallas_call(
        _mlp_kernel,
        grid=(n // TM,),
        in_specs=[pl.BlockSpec((TM, d), lambda i: (i, 0)),
                  pl.BlockSpec((TM, PLE_DIM), lambda i: (i, 0))]
                 + [_const_spec(c.shape) for c in consts],
        out_specs=pl.BlockSpec((TM, d), lambda i: (i, 0)),
        out_shape=jax.ShapeDtypeStruct((n, d), F32),
        compiler_params=pltpu.CompilerParams(
            dimension_semantics=("arbitrary",), vmem_limit_bytes=VMEM_LIMIT),
        name="mlp",
    )(h, p, *consts)


def kernel(x, p, mix_norm_g, w_in, ssd_conv_w, ssd_conv_b, dt_bias, A_log, D_skip, ssd_norm_g,
           conf_dw_w, conf_dw_b, conf_ln_g, conf_ln_b, w_out, mlp_norm_g, w_up, w_down,
           ple_gate_norm_g, w_ple_gate, b_ple_gate, w_ple, ple_norm_g, final_norm_g):
    b, l, d = x.shape
    assert (d, l % TL, (b * l) % TM, mix_norm_g.shape[0]) == (D_MODEL, 0, 0, 1)
    h = _mixer(x, mix_norm_g[0], w_in[0], ssd_conv_w[0], ssd_conv_b[0], dt_bias[0], A_log[0],
               D_skip[0], ssd_norm_g[0], conf_dw_w[0], conf_dw_b[0], conf_ln_g[0], conf_ln_b[0],
               w_out[0])
    out = _mlp(h.reshape(b * l, d), p[0].reshape(b * l, PLE_DIM), mlp_norm_g[0], w_up[0],
               w_down[0], ple_gate_norm_g[0], w_ple_gate[0], b_ple_gate[0], w_ple[0],
               ple_norm_g[0], final_norm_g)
    return out.reshape(b, l, d)
```

```python
import math

import numpy as np
import jax
import jax.numpy as jnp
from jax import lax
from jax.experimental import pallas as pl
from jax.experimental.pallas import tpu as pltpu

D_MODEL = 1024
SSD_WIDTH = 1024
SSD_HEAD_DIM = 64
SSD_HEADS = 16
SSD_GROUPS = 2
SSD_STATE = 128
SSD_CONV = 4
CHUNK = 128
CONF_WIDTH = 1024
CONF_KERNEL = 31
D_FF = 4096
PLE_DIM = 256
EPS = 1e-6
XBC_WIDTH = SSD_WIDTH + 2 * SSD_GROUPS * SSD_STATE
GROUP_WIDTH = SSD_WIDTH // SSD_GROUPS
HEADS_PER_GROUP = SSD_HEADS // SSD_GROUPS

LANES = 128
SUBLANES = 8
SPLIT = 3
SSD_HALO = SUBLANES
CONF_HALO = 32
PACK = 16
CONV_ROWS = 64

TL = 512
TM = 1024
VMEM_LIMIT = 56 * 1024 * 1024
LOG2E = math.log2(math.e)

F32 = jnp.float32
BF16 = jnp.bfloat16


def _sigmoid(x):
    return 1.0 / (1.0 + jnp.exp(-x))


def _silu(x):
    return x * _sigmoid(x)


def _softplus(x):
    return jnp.maximum(x, 0.0) + jnp.log1p(jnp.exp(-jnp.abs(x)))


def _rmsnorm(x, g):
    return x * lax.rsqrt(jnp.mean(x * x, axis=-1, keepdims=True) + EPS) * g


def _dot(a, b):
    return jnp.dot(a, b, preferred_element_type=F32)


def _split3_cols(v_t):
    hi = v_t.astype(BF16).astype(F32)
    r1 = v_t - hi
    mid = r1.astype(BF16).astype(F32)
    lo = r1 - mid
    pad = jnp.zeros((LANES - SPLIT * SSD_HEADS, v_t.shape[1]), F32)
    return jnp.concatenate([hi, mid, lo, pad], axis=0).T.astype(BF16)


def _cumsum_lanes(a, lane):
    s = 1
    while s < a.shape[1]:
        a = a + jnp.where(lane >= s, pltpu.roll(a, s, axis=1), 0.0)
        s *= 2
    return a


def _conv31(vbuf, shbuf, pbuf, cw_ref, cb_ref, cbuf, rows):
    first = CONF_HALO - (CONF_KERNEL - 1)
    span = rows + CONF_HALO - SUBLANES
    half = CONF_KERNEL // 2

    def lane_block(cbk, carry):
        cs = pl.ds(pl.multiple_of(cbk * LANES, LANES), LANES)
        for r in range(1, SUBLANES):
            shbuf[r - 1] = vbuf[pl.ds(r, span), cs]
        for r in range(PACK):
            q, r8 = divmod(r, SUBLANES)
            n = rows + (CONF_HALO if r == 0 else PACK)
            src = shbuf[r8 - 1, pl.ds(q * SUBLANES, n), :] if r8 else vbuf[pl.ds(q * SUBLANES, n), cs]
            pbuf[r, 0:n, :] = src.astype(BF16)
        for rb in range(0, rows, CONV_ROWS):
            acc = [None, None]
            for k in range(CONF_KERNEL):
                a, r = divmod(first + k, PACK)
                win = pbuf[r, pl.ds(a * PACK + rb, CONV_ROWS), :].reshape(CONV_ROWS // PACK, PACK, LANES)
                tap = pltpu.bitcast(jnp.broadcast_to(cw_ref[k:k + 1, cs], (SUBLANES, LANES)), BF16)[None]
                i = k // (half + 1)
                acc[i] = win * tap if acc[i] is None else acc[i] + win * tap
            out = acc[0].astype(F32) + acc[1].astype(F32)
            cbuf[rb:rb + CONV_ROWS, cs] = out.reshape(CONV_ROWS, LANES) + cb_ref[0:1, cs]
        return carry

    lax.fori_loop(0, CONF_WIDTH // LANES, lane_block, 0)


def _mixer_kernel(x_ref, g_ref, wz_ref, wxbc_ref, wdt_ref, wcv_ref, wcg_ref,
                  scw_ref, scb_ref, dtb_ref, aneg_ref, dskip_ref, sng_ref,
                  cw_ref, cb_ref, lng_ref, lnb_ref, wout_ref, e128_ref, e64_ref,
                  out_ref, xbuf, vbuf, shbuf, pbuf, cbuf, ybuf, state):
    t = pl.program_id(1)

    @pl.when(t == 0)
    def _():
        xbuf[:, TL:TL + SSD_HALO, :] = jnp.zeros((SSD_CONV - 1, SSD_HALO, XBC_WIDTH), F32)
        vbuf[0:CONF_HALO, :] = jnp.zeros((CONF_HALO, CONF_WIDTH), F32)
        state[...] = jnp.zeros(state.shape, F32)

    x = x_ref[...]
    ub = _rmsnorm(x, g_ref[...]).astype(BF16)

    pre = _dot(ub, wxbc_ref[...])
    conv = scb_ref[...] + scw_ref[SSD_CONV - 1:SSD_CONV, :] * pre
    for k in range(SSD_CONV - 1):
        xbuf[k, 0:SSD_HALO, :] = xbuf[k, TL:TL + SSD_HALO, :]
        xbuf[k, pl.ds(SSD_CONV - 1 - k, TL), :] = pre
        conv = conv + scw_ref[k:k + 1, :] * xbuf[k, 0:TL, :]
    xbc = _silu(conv)

    dt = _softplus(_dot(ub, wdt_ref[...]) + dtb_ref[...])

    lane = lax.broadcasted_iota(jnp.int32, (CHUNK, LANES), 1)
    row = lax.broadcasted_iota(jnp.int32, (CHUNK, LANES), 0)
    causal = row >= lane
    low_half = lane < SSD_HEAD_DIM
    lane_h = lax.broadcasted_iota(jnp.int32, (SSD_HEADS, CHUNK), 1)

    for c in range(TL // CHUNK):
        r0 = c * CHUNK
        xs = xbc[r0:r0 + CHUNK, :SSD_WIDTH]
        xs_bf = xs.astype(BF16)
        b_bf = xbc[r0:r0 + CHUNK, SSD_WIDTH:SSD_WIDTH + SSD_GROUPS * SSD_STATE].astype(BF16)
        c_bf = xbc[r0:r0 + CHUNK, SSD_WIDTH + SSD_GROUPS * SSD_STATE:].astype(BF16)

        dt_t = dt[r0:r0 + CHUNK, :].T[0:SSD_HEADS]
        acs_t = _cumsum_lanes(dt_t * aneg_ref[...], lane_h)
        acs2_t = acs_t * LOG2E
        rows_t = acs2_t - jnp.log2(dt_t)
        w_t = dt_t * jnp.exp(acs_t[:, CHUNK - 1:CHUNK] - acs_t)

        colb = _dot(_split3_cols(acs2_t), e128_ref[...])
        w64 = _dot(_split3_cols(w_t), e64_ref[...])

        e64_blocks = []
        y_blocks = []
        for g in range(SSD_GROUPS):
            bg = b_bf[:, g * SSD_STATE:(g + 1) * SSD_STATE]
            cg = c_bf[:, g * SSD_STATE:(g + 1) * SSD_STATE]
            scores = lax.dot_general(cg, bg, (((1,), (1,)), ((), ())),
                                     preferred_element_type=F32).astype(BF16)
            for jp in range(HEADS_PER_GROUP // 2):
                j = g * (HEADS_PER_GROUP // 2) + jp
                ms = []
                for h in (2 * j, 2 * j + 1):
                    seg = colb[:, h * LANES:(h + 1) * LANES] - rows_t[h:h + 1, :]
                    lmat = jnp.where(causal, jnp.exp2(seg), 0.0).astype(BF16)
                    ms.append(lmat * scores)
                xp = xs_bf[:, j * LANES:(j + 1) * LANES]
                zero = jnp.zeros_like(xp)
                rhs = jnp.concatenate([jnp.where(low_half, xp, zero),
                                       jnp.where(low_half, zero, xp)], axis=0)
                y_blocks.append(_dot(jnp.concatenate(ms, axis=1), rhs))
                e64_blocks.append(jnp.exp2(jnp.where(
                    low_half, colb[:, (2 * j) * LANES:(2 * j + 1) * LANES],
                    colb[:, (2 * j + 1) * LANES:(2 * j + 2) * LANES])))
        e64 = jnp.concatenate(e64_blocks, axis=1)
        y_diag = jnp.concatenate(y_blocks, axis=1)

        xd_bf = (xs * w64).astype(BF16)
        y_off = []
        for g in range(SSD_GROUPS):
            gs = slice(g * GROUP_WIDTH, (g + 1) * GROUP_WIDTH)
            bg = b_bf[:, g * SSD_STATE:(g + 1) * SSD_STATE]
            cg = c_bf[:, g * SSD_STATE:(g + 1) * SSD_STATE]
            s_old = state[g]
            y_off.append(_dot(cg, s_old.astype(BF16)) * e64[:, gs])
            new = lax.dot_general(bg, xd_bf[:, gs], (((0,), (0,)), ((), ())),
                                  preferred_element_type=F32)
            state[g] = s_old * e64[CHUNK - 1:CHUNK, gs] + new
        ybuf[r0:r0 + CHUNK, :] = y_diag + jnp.concatenate(y_off, axis=1) + xs * dskip_ref[...]

    z = _dot(ub, wz_ref[...])
    v = ybuf[...] * _silu(z)
    parts = []
    for g in range(SSD_GROUPS):
        vg = v[:, g * GROUP_WIDTH:(g + 1) * GROUP_WIDTH]
        parts.append(vg * lax.rsqrt(jnp.mean(vg * vg, axis=-1, keepdims=True) + EPS))
    y_ssd = (jnp.concatenate(parts, axis=1) * sng_ref[...]).astype(BF16)

    vbuf[CONF_HALO:CONF_HALO + TL, :] = _dot(ub, wcv_ref[...]) * _sigmoid(_dot(ub, wcg_ref[...]))
    _conv31(vbuf, shbuf, pbuf, cw_ref, cb_ref, cbuf, TL)
    vbuf[0:CONF_HALO, :] = vbuf[TL:TL + CONF_HALO, :]

    cv = cbuf[...]
    mu = jnp.mean(cv, axis=-1, keepdims=True)
    xc = cv - mu
    ln = xc * lax.rsqrt(jnp.mean(xc * xc, axis=-1, keepdims=True) + EPS) * lng_ref[...] + lnb_ref[...]
    y_conf = _silu(ln).astype(BF16)

    out_ref[...] = (x + _dot(y_ssd, wout_ref[0:SSD_WIDTH, :])
                    + _dot(y_conf, wout_ref[SSD_WIDTH:, :]))


def _mlp_kernel(h_ref, p_ref, g1_ref, wup_ref, wdown_ref, g2_ref, wgate_ref, bgate_ref,
                wple_ref, g3_ref, gf_ref, out_ref):
    h = h_ref[...]
    u = _rmsnorm(h, g1_ref[...]).astype(BF16)
    hid = jnp.maximum(_dot(u, wup_ref[...]), 0.0)
    h = h + _dot((hid * hid).astype(BF16), wdown_ref[...])
    u = _rmsnorm(h, g2_ref[...]).astype(BF16)
    gate = _sigmoid(_dot(u, wgate_ref[...]) + bgate_ref[...])
    emb = _rmsnorm(_dot(p_ref[...].astype(BF16), wple_ref[...]), g3_ref[...])
    out_ref[...] = _rmsnorm(h + gate * emb, gf_ref[...])


def _const_spec(shape):
    nd = len(shape)
    return pl.BlockSpec(shape, lambda *_: (0,) * nd, pipeline_mode=pl.Buffered(1))


def _expand_matrix(width):
    e = np.zeros((LANES, SSD_HEADS * width), np.float32)
    for piece in range(SPLIT):
        for h in range(SSD_HEADS):
            e[piece * SSD_HEADS + h, h * width:(h + 1) * width] = 1.0
    return jnp.asarray(e, BF16)


def _pair_bf16(w):
    bits = lax.bitcast_convert_type(w.astype(BF16), jnp.uint16).astype(jnp.uint32)
    return bits | (bits << 16)


def _mixer(x, mix_norm_g, w_in, ssd_conv_w, ssd_conv_b, dt_bias, A_log, D_skip, ssd_norm_g,
           conf_dw_w, conf_dw_b, conf_ln_g, conf_ln_b, w_out):
    b, l, d = x.shape
    o = np.cumsum([SSD_WIDTH, XBC_WIDTH, SSD_HEADS, CONF_WIDTH]).tolist()
    wb = w_in.astype(BF16)
    w_z, w_xbc, w_dt, w_cv, w_cg = (wb[:, :o[0]], wb[:, o[0]:o[1]], wb[:, o[1]:o[2]],
                                    wb[:, o[2]:o[3]], wb[:, o[3]:])
    pad_heads = lambda v: jnp.pad(v, ((0, 0), (0, LANES - SSD_HEADS)))
    row = lambda v: v.astype(F32)[None, :]
    consts = [
        row(mix_norm_g), w_z, w_xbc, pad_heads(w_dt), w_cv, w_cg,
        ssd_conv_w.astype(F32), row(ssd_conv_b), pad_heads(row(dt_bias)),
        -jnp.exp(A_log.astype(F32))[:, None],
        row(jnp.repeat(D_skip, SSD_HEAD_DIM)), row(ssd_norm_g),
        _pair_bf16(conf_dw_w),
        row(conf_dw_b), row(conf_ln_g), row(conf_ln_b), w_out.astype(BF16),
        _expand_matrix(LANES), _expand_matrix(SSD_HEAD_DIM),
    ]
    tile = pl.BlockSpec((None, TL, d), lambda bi, ti: (bi, ti, 0))
    return pl.pallas_call(
        _mixer_kernel,
        grid=(b, l // TL),
        in_specs=[tile] + [_const_spec(c.shape) for c in consts],
        out_specs=tile,
        out_shape=jax.ShapeDtypeStruct((b, l, d), F32),
        scratch_shapes=[
            pltpu.VMEM((SSD_CONV - 1, TL + SSD_HALO, XBC_WIDTH), F32),
            pltpu.VMEM((CONF_HALO + TL, CONF_WIDTH), F32),
            pltpu.VMEM((SUBLANES - 1, TL + CONF_HALO - SUBLANES, LANES), F32),
            pltpu.VMEM((PACK, TL + CONF_HALO, LANES), BF16),
            pltpu.VMEM((TL, CONF_WIDTH), F32),
            pltpu.VMEM((TL, SSD_WIDTH), F32),
            pltpu.VMEM((SSD_GROUPS, SSD_STATE, GROUP_WIDTH), F32),
        ],
        compiler_params=pltpu.CompilerParams(
            dimension_semantics=("arbitrary", "arbitrary"), vmem_limit_bytes=VMEM_LIMIT),
        name="mixer",
    )(x, *consts)


def _mlp(h, p, mlp_norm_g, w_up, w_down, ple_gate_norm_g, w_ple_gate, b_ple_gate, w_ple,
         ple_norm_g, final_norm_g):
    n, d = h.shape
    row = lambda v: v.astype(F32)[None, :]
    consts = [row(mlp_norm_g), w_up.astype(BF16), w_down.astype(BF16), row(ple_gate_norm_g),
              w_ple_gate.astype(BF16), row(b_ple_gate), w_ple.astype(BF16), row(ple_norm_g),
              row(final_norm_g)]
    return pl.pallas_call(
        _mlp_kernel,
        grid=(n // TM,),
        in_specs=[pl.BlockSpec((TM, d), lambda i: (i, 0)),
                  pl.BlockSpec((TM, PLE_DIM), lambda i: (i, 0))]
                 + [_const_spec(c.shape) for c in consts],
        out_specs=pl.BlockSpec((TM, d), lambda i: (i, 0)),
        out_shape=jax.ShapeDtypeStruct((n, d), F32),
        compiler_params=pltpu.CompilerParams(
            dimension_semantics=("arbitrary",), vmem_limit_bytes=VMEM_LIMIT),
        name="mlp",
    )(h, p, *consts)


def kernel(x, p, mix_norm_g, w_in, ssd_conv_w, ssd_conv_b, dt_bias, A_log, D_skip, ssd_norm_g,
           conf_dw_w, conf_dw_b, conf_ln_g, conf_ln_b, w_out, mlp_norm_g, w_up, w_down,
           ple_gate_norm_g, w_ple_gate, b_ple_gate, w_ple, ple_norm_g, final_norm_g):
    b, l, d = x.shape
    assert (d, l % TL, (b * l) % TM, mix_norm_g.shape[0]) == (D_MODEL, 0, 0, 1)
    h = _mixer(x, mix_norm_g[0], w_in[0], ssd_conv_w[0], ssd_conv_b[0], dt_bias[0], A_log[0],
               D_skip[0], ssd_norm_g[0], conf_dw_w[0], conf_dw_b[0], conf_ln_g[0], conf_ln_b[0],
               w_out[0])
    out = _mlp(h.reshape(b * l, d), p[0].reshape(b * l, PLE_DIM), mlp_norm_g[0], w_up[0],
               w_down[0], ple_gate_norm_g[0], w_ple_gate[0], b_ple_gate[0], w_ple[0],
               ple_norm_g[0], final_norm_g)
    return out.reshape(b, l, d)
```

```python
import math

import numpy as np
import jax
import jax.numpy as jnp
from jax import lax
from jax.experimental import pallas as pl
from jax.experimental.pallas import tpu as pltpu

D_MODEL = 1024
SSD_WIDTH = 1024
SSD_HEAD_DIM = 64
SSD_HEADS = 16
SSD_GROUPS = 2
SSD_STATE = 128
SSD_CONV = 4
CHUNK = 128
CONF_WIDTH = 1024
CONF_KERNEL = 31
D_FF = 4096
PLE_DIM = 256
EPS = 1e-6
XBC_WIDTH = SSD_WIDTH + 2 * SSD_GROUPS * SSD_STATE
GROUP_WIDTH = SSD_WIDTH // SSD_GROUPS
HEADS_PER_GROUP = SSD_HEADS // SSD_GROUPS

LANES = 128
SUBLANES = 8
SPLIT = 3
SSD_HALO = SUBLANES
CONF_HALO = 32
CONV_ROWS = 32

TL = 512
TM = 1024
VMEM_LIMIT = 56 * 1024 * 1024
LOG2E = math.log2(math.e)

F32 = jnp.float32
BF16 = jnp.bfloat16


def _sigmoid(x):
    return 1.0 / (1.0 + jnp.exp(-x))


def _silu(x):
    return x * _sigmoid(x)


def _softplus(x):
    return jnp.maximum(x, 0.0) + jnp.log1p(jnp.exp(-jnp.abs(x)))


def _rmsnorm(x, g):
    return x * lax.rsqrt(jnp.mean(x * x, axis=-1, keepdims=True) + EPS) * g


def _dot(a, b):
    return jnp.dot(a, b, preferred_element_type=F32)


def _split3_cols(v_t):
    hi = v_t.astype(BF16).astype(F32)
    r1 = v_t - hi
    mid = r1.astype(BF16).astype(F32)
    lo = r1 - mid
    pad = jnp.zeros((LANES - SPLIT * SSD_HEADS, v_t.shape[1]), F32)
    return jnp.concatenate([hi, mid, lo, pad], axis=0).T.astype(BF16)


def _cumsum_lanes(a, lane):
    s = 1
    while s < a.shape[1]:
        a = a + jnp.where(lane >= s, pltpu.roll(a, s, axis=1), 0.0)
        s *= 2
    return a


def _conv31_copies(vbuf, shbuf, rows, first_blk):
    span = rows + CONF_HALO - SUBLANES
    for j in range(shbuf.shape[0]):
        cs = slice((first_blk + j) * LANES, (first_blk + j + 1) * LANES)
        for r in range(1, SUBLANES):
            shbuf[j, r - 1] = vbuf[pl.ds(r, span), cs]


def _conv31_taps(vbuf, shbuf, cw_ref, cb_ref, cbuf, rows, first_blk):
    first = CONF_HALO - (CONF_KERNEL - 1)

    def lane_block(j, carry):
        cs = pl.ds(pl.multiple_of((first_blk + j) * LANES, LANES), LANES)
        for rb in range(0, rows, CONV_ROWS):
            acc = jnp.broadcast_to(cb_ref[0:1, cs], (CONV_ROWS, LANES))
            for k in range(CONF_KERNEL):
                q, r = divmod(first + k, SUBLANES)
                lo = q * SUBLANES + rb
                win = shbuf[j, r - 1, pl.ds(lo, CONV_ROWS), :] if r else vbuf[pl.ds(lo, CONV_ROWS), cs]
                acc = acc + win * cw_ref[k:k + 1, cs]
            cbuf[rb:rb + CONV_ROWS, cs] = acc
        return carry

    lax.fori_loop(0, shbuf.shape[0], lane_block, 0)


def _mixer_kernel(x_ref, g_ref, wz_ref, wxbc_ref, wdt_ref, wcv_ref, wcg_ref,
                  scw_ref, scb_ref, dtb_ref, aneg_ref, dskip_ref, sng_ref,
                  cw_ref, cb_ref, lng_ref, lnb_ref, wout_ref, e128_ref, e64_ref,
                  out_ref, xbuf, vbuf, shbuf, cbuf, ybuf, state):
    t = pl.program_id(1)

    @pl.when(t == 0)
    def _():
        xbuf[:, TL:TL + SSD_HALO, :] = jnp.zeros((SSD_CONV - 1, SSD_HALO, XBC_WIDTH), F32)
        vbuf[0:CONF_HALO, :] = jnp.zeros((CONF_HALO, CONF_WIDTH), F32)
        state[...] = jnp.zeros(state.shape, F32)

    x = x_ref[...]
    ub = _rmsnorm(x, g_ref[...]).astype(BF16)

    vbuf[CONF_HALO:CONF_HALO + TL, :] = _dot(ub, wcv_ref[...]) * _sigmoid(_dot(ub, wcg_ref[...]))
    half_blocks = shbuf.shape[0]
    _conv31_copies(vbuf, shbuf, TL, 0)
    _conv31_taps(vbuf, shbuf, cw_ref, cb_ref, cbuf, TL, 0)
    _conv31_copies(vbuf, shbuf, TL, half_blocks)

    pre = _dot(ub, wxbc_ref[...])
    conv = scb_ref[...] + scw_ref[SSD_CONV - 1:SSD_CONV, :] * pre
    for k in range(SSD_CONV - 1):
        xbuf[k, 0:SSD_HALO, :] = xbuf[k, TL:TL + SSD_HALO, :]
        xbuf[k, pl.ds(SSD_CONV - 1 - k, TL), :] = pre
        conv = conv + scw_ref[k:k + 1, :] * xbuf[k, 0:TL, :]
    xbc = _silu(conv)

    dt = _softplus(_dot(ub, wdt_ref[...]) + dtb_ref[...])

    lane = lax.broadcasted_iota(jnp.int32, (CHUNK, LANES), 1)
    row = lax.broadcasted_iota(jnp.int32, (CHUNK, LANES), 0)
    causal = row >= lane
    low_half = lane < SSD_HEAD_DIM
    lane_h = lax.broadcasted_iota(jnp.int32, (SSD_HEADS, CHUNK), 1)

    for c in range(TL // CHUNK):
        r0 = c * CHUNK
        xs = xbc[r0:r0 + CHUNK, :SSD_WIDTH]
        xs_bf = xs.astype(BF16)
        b_bf = xbc[r0:r0 + CHUNK, SSD_WIDTH:SSD_WIDTH + SSD_GROUPS * SSD_STATE].astype(BF16)
        c_bf = xbc[r0:r0 + CHUNK, SSD_WIDTH + SSD_GROUPS * SSD_STATE:].astype(BF16)

        dt_t = dt[r0:r0 + CHUNK, :].T[0:SSD_HEADS]
        acs_t = _cumsum_lanes(dt_t * aneg_ref[...], lane_h)
        acs2_t = acs_t * LOG2E
        rows_t = acs2_t - jnp.log2(dt_t)
        w_t = dt_t * jnp.exp(acs_t[:, CHUNK - 1:CHUNK] - acs_t)

        colb = _dot(_split3_cols(acs2_t), e128_ref[...])
        w64 = _dot(_split3_cols(w_t), e64_ref[...])

        e64_blocks = []
        y_blocks = []
        for g in range(SSD_GROUPS):
            bg = b_bf[:, g * SSD_STATE:(g + 1) * SSD_STATE]
            cg = c_bf[:, g * SSD_STATE:(g + 1) * SSD_STATE]
            scores = lax.dot_general(cg, bg, (((1,), (1,)), ((), ())),
                                     preferred_element_type=F32).astype(BF16)
            for jp in range(HEADS_PER_GROUP // 2):
                j = g * (HEADS_PER_GROUP // 2) + jp
                ms = []
                for h in (2 * j, 2 * j + 1):
                    seg = colb[:, h * LANES:(h + 1) * LANES] - rows_t[h:h + 1, :]
                    lmat = jnp.where(causal, jnp.exp2(seg), 0.0).astype(BF16)
                    ms.append(lmat * scores)
                xp = xs_bf[:, j * LANES:(j + 1) * LANES]
                zero = jnp.zeros_like(xp)
                rhs = jnp.concatenate([jnp.where(low_half, xp, zero),
                                       jnp.where(low_half, zero, xp)], axis=0)
                y_blocks.append(_dot(jnp.concatenate(ms, axis=1), rhs))
                e64_blocks.append(jnp.exp2(jnp.where(
                    low_half, colb[:, (2 * j) * LANES:(2 * j + 1) * LANES],
                    colb[:, (2 * j + 1) * LANES:(2 * j + 2) * LANES])))
        e64 = jnp.concatenate(e64_blocks, axis=1)
        y_diag = jnp.concatenate(y_blocks, axis=1)

        xd_bf = (xs * w64).astype(BF16)
        y_off = []
        for g in range(SSD_GROUPS):
            gs = slice(g * GROUP_WIDTH, (g + 1) * GROUP_WIDTH)
            bg = b_bf[:, g * SSD_STATE:(g + 1) * SSD_STATE]
            cg = c_bf[:, g * SSD_STATE:(g + 1) * SSD_STATE]
            s_old = state[g]
            y_off.append(_dot(cg, s_old.astype(BF16)) * e64[:, gs])
            new = lax.dot_general(bg, xd_bf[:, gs], (((0,), (0,)), ((), ())),
                                  preferred_element_type=F32)
            state[g] = s_old * e64[CHUNK - 1:CHUNK, gs] + new
        ybuf[r0:r0 + CHUNK, :] = y_diag + jnp.concatenate(y_off, axis=1) + xs * dskip_ref[...]

    z = _dot(ub, wz_ref[...])
    v = ybuf[...] * _silu(z)
    parts = []
    for g in range(SSD_GROUPS):
        vg = v[:, g * GROUP_WIDTH:(g + 1) * GROUP_WIDTH]
        parts.append(vg * lax.rsqrt(jnp.mean(vg * vg, axis=-1, keepdims=True) + EPS))
    y_ssd = (jnp.concatenate(parts, axis=1) * sng_ref[...]).astype(BF16)

    _conv31_taps(vbuf, shbuf, cw_ref, cb_ref, cbuf, TL, half_blocks)
    vbuf[0:CONF_HALO, :] = vbuf[TL:TL + CONF_HALO, :]

    cv = cbuf[...]
    mu = jnp.mean(cv, axis=-1, keepdims=True)
    xc = cv - mu
    ln = xc * lax.rsqrt(jnp.mean(xc * xc, axis=-1, keepdims=True) + EPS) * lng_ref[...] + lnb_ref[...]
    y_conf = _silu(ln).astype(BF16)

    out_ref[...] = (x + _dot(y_ssd, wout_ref[0:SSD_WIDTH, :])
                    + _dot(y_conf, wout_ref[SSD_WIDTH:, :]))


def _mlp_kernel(h_ref, p_ref, g1_ref, wup_ref, wdown_ref, g2_ref, wgate_ref, bgate_ref,
                wple_ref, g3_ref, gf_ref, out_ref):
    h = h_ref[...]
    u = _rmsnorm(h, g1_ref[...]).astype(BF16)
    hid = jnp.maximum(_dot(u, wup_ref[...]), 0.0)
    h = h + _dot((hid * hid).astype(BF16), wdown_ref[...])
    u = _rmsnorm(h, g2_ref[...]).astype(BF16)
    gate = _sigmoid(_dot(u, wgate_ref[...]) + bgate_ref[...])
    emb = _rmsnorm(_dot(p_ref[...].astype(BF16), wple_ref[...]), g3_ref[...])
    out_ref[...] = _rmsnorm(h + gate * emb, gf_ref[...])


def _const_spec(shape):
    nd = len(shape)
    return pl.BlockSpec(shape, lambda *_: (0,) * nd, pipeline_mode=pl.Buffered(1))


def _expand_matrix(width):
    e = np.zeros((LANES, SSD_HEADS * width), np.float32)
    for piece in range(SPLIT):
        for h in range(SSD_HEADS):
            e[piece * SSD_HEADS + h, h * width:(h + 1) * width] = 1.0
    return jnp.asarray(e, BF16)


def _mixer(x, mix_norm_g, w_in, ssd_conv_w, ssd_conv_b, dt_bias, A_log, D_skip, ssd_norm_g,
           conf_dw_w, conf_dw_b, conf_ln_g, conf_ln_b, w_out):
    b, l, d = x.shape
    o = np.cumsum([SSD_WIDTH, XBC_WIDTH, SSD_HEADS, CONF_WIDTH]).tolist()
    wb = w_in.astype(BF16)
    w_z, w_xbc, w_dt, w_cv, w_cg = (wb[:, :o[0]], wb[:, o[0]:o[1]], wb[:, o[1]:o[2]],
                                    wb[:, o[2]:o[3]], wb[:, o[3]:])
    pad_heads = lambda v: jnp.pad(v, ((0, 0), (0, LANES - SSD_HEADS)))
    row = lambda v: v.astype(F32)[None, :]
    consts = [
        row(mix_norm_g), w_z, w_xbc, pad_heads(w_dt), w_cv, w_cg,
        ssd_conv_w.astype(F32), row(ssd_conv_b), pad_heads(row(dt_bias)),
        -jnp.exp(A_log.astype(F32))[:, None],
        row(jnp.repeat(D_skip, SSD_HEAD_DIM)), row(ssd_norm_g),
        conf_dw_w.astype(F32), row(conf_dw_b), row(conf_ln_g), row(conf_ln_b), w_out.astype(BF16),
        _expand_matrix(LANES), _expand_matrix(SSD_HEAD_DIM),
    ]
    tile = pl.BlockSpec((None, TL, d), lambda bi, ti: (bi, ti, 0))
    return pl.pallas_call(
        _mixer_kernel,
        grid=(b, l // TL),
        in_specs=[tile] + [_const_spec(c.shape) for c in consts],
        out_specs=tile,
        out_shape=jax.ShapeDtypeStruct((b, l, d), F32),
        scratch_shapes=[
            pltpu.VMEM((SSD_CONV - 1, TL + SSD_HALO, XBC_WIDTH), F32),
            pltpu.VMEM((CONF_HALO + TL, CONF_WIDTH), F32),
            pltpu.VMEM((CONF_WIDTH // LANES // 2, SUBLANES - 1, TL + CONF_HALO - SUBLANES, LANES), F32),
            pltpu.VMEM((TL, CONF_WIDTH), F32),
            pltpu.VMEM((TL, SSD_WIDTH), F32),
            pltpu.VMEM((SSD_GROUPS, SSD_STATE, GROUP_WIDTH), F32),
        ],
        compiler_params=pltpu.CompilerParams(
            dimension_semantics=("arbitrary", "arbitrary"), vmem_limit_bytes=VMEM_LIMIT),
        name="mixer",
    )(x, *consts)


def _mlp(h, p, mlp_norm_g, w_up, w_down, ple_gate_norm_g, w_ple_gate, b_ple_gate, w_ple,
         ple_norm_g, final_norm_g):
    n, d = h.shape
    row = lambda v: v.astype(F32)[None, :]
    consts = [row(mlp_norm_g), w_up.astype(BF16), w_down.astype(BF16), row(ple_gate_norm_g),
              w_ple_gate.astype(BF16), row(b_ple_gate), w_ple.astype(BF16), row(ple_norm_g),
              row(final_norm_g)]
    return pl.pallas_call(
        _mlp_kernel,
        grid=(n // TM,),
        in_specs=[pl.BlockSpec((TM, d), lambda i: (i, 0)),
                  pl.BlockSpec((TM, PLE_DIM), lambda i: (i, 0))]
                 + [_const_spec(c.shape) for c in consts],
        out_specs=pl.BlockSpec((TM, d), lambda i: (i, 0)),
        out_shape=jax.ShapeDtypeStruct((n, d), F32),
        compiler_params=pltpu.CompilerParams(
            dimension_semantics=("arbitrary",), vmem_limit_bytes=VMEM_LIMIT),
        name="mlp",
    )(h, p, *consts)


def kernel(x, p, mix_norm_g, w_in, ssd_conv_w, ssd_conv_b, dt_bias, A_log, D_skip, ssd_norm_g,
           conf_dw_w, conf_dw_b, conf_ln_g, conf_ln_b, w_out, mlp_norm_g, w_up, w_down,
           ple_gate_norm_g, w_ple_gate, b_ple_gate, w_ple, ple_norm_g, final_norm_g):
    b, l, d = x.shape
    assert (d, l % TL, (b * l) % TM, mix_norm_g.shape[0]) == (D_MODEL, 0, 0, 1)
    h = _mixer(x, mix_norm_g[0], w_in[0], ssd_conv_w[0], ssd_conv_b[0], dt_bias[0], A_log[0],
               D_skip[0], ssd_norm_g[0], conf_dw_w[0], conf_dw_b[0], conf_ln_g[0], conf_ln_b[0],
               w_out[0])
    out = _mlp(h.reshape(b * l, d), p[0].reshape(b * l, PLE_DIM), mlp_norm_g[0], w_up[0],
               w_down[0], ple_gate_norm_g[0], w_ple_gate[0], b_ple_gate[0], w_ple[0],
               ple_norm_g[0], final_norm_g)
    return out.reshape(b, l, d)
```

```python
import math

import numpy as np
import jax
import jax.numpy as jnp
from jax import lax
from jax.experimental import pallas as pl
from jax.experimental.pallas import tpu as pltpu

D_MODEL = 1024
SSD_WIDTH = 1024
SSD_HEAD_DIM = 64
SSD_HEADS = 16
SSD_GROUPS = 2
SSD_STATE = 128
SSD_CONV = 4
CHUNK = 128
CONF_WIDTH = 1024
CONF_KERNEL = 31
D_FF = 4096
PLE_DIM = 256
EPS = 1e-6
XBC_WIDTH = SSD_WIDTH + 2 * SSD_GROUPS * SSD_STATE
GROUP_WIDTH = SSD_WIDTH // SSD_GROUPS
HEADS_PER_GROUP = SSD_HEADS // SSD_GROUPS

LANES = 128
SUBLANES = 8
SPLIT = 3
SSD_HALO = SUBLANES
CONF_HALO = 32
CONV_ROWS = 32

TL = 512
TM = 1024
VMEM_LIMIT = 56 * 1024 * 1024
LOG2E = math.log2(math.e)

F32 = jnp.float32
BF16 = jnp.bfloat16


def _sigmoid(x):
    return 1.0 / (1.0 + jnp.exp(-x))


def _silu(x):
    h = 0.5 * x
    return h + h * jnp.tanh(h)


def _softplus(x):
    return jnp.maximum(x, 0.0) + jnp.log1p(jnp.exp(-jnp.abs(x)))


def _rmsnorm(x, g):
    return x * lax.rsqrt(jnp.mean(x * x, axis=-1, keepdims=True) + EPS) * g


def _dot(a, b):
    return jnp.dot(a, b, preferred_element_type=F32)


def _split3_cols(v_t):
    hi = v_t.astype(BF16).astype(F32)
    r1 = v_t - hi
    mid = r1.astype(BF16).astype(F32)
    lo = r1 - mid
    pad = jnp.zeros((LANES - SPLIT * SSD_HEADS, v_t.shape[1]), F32)
    return jnp.concatenate([hi, mid, lo, pad], axis=0).T.astype(BF16)


def _cumsum_lanes(a, lane):
    s = 1
    while s < a.shape[1]:
        a = a + jnp.where(lane >= s, pltpu.roll(a, s, axis=1), 0.0)
        s *= 2
    return a


def _conv31_copies(vbuf, shbuf, rows, first_blk):
    span = rows + CONF_HALO - SUBLANES
    for j in range(shbuf.shape[0]):
        cs = slice((first_blk + j) * LANES, (first_blk + j + 1) * LANES)
        for r in range(1, SUBLANES):
            shbuf[j, r - 1] = vbuf[pl.ds(r, span), cs]


def _conv31_taps(vbuf, shbuf, cw_ref, cb_ref, cbuf, rows, first_blk):
    first = CONF_HALO - (CONF_KERNEL - 1)

    def lane_block(j, carry):
        cs = pl.ds(pl.multiple_of((first_blk + j) * LANES, LANES), LANES)
        for rb in range(0, rows, CONV_ROWS):
            acc = jnp.broadcast_to(cb_ref[0:1, cs], (CONV_ROWS, LANES))
            for k in range(CONF_KERNEL):
                q, r = divmod(first + k, SUBLANES)
                lo = q * SUBLANES + rb
                win = shbuf[j, r - 1, pl.ds(lo, CONV_ROWS), :] if r else vbuf[pl.ds(lo, CONV_ROWS), cs]
                acc = acc + win * cw_ref[k:k + 1, cs]
            cbuf[rb:rb + CONV_ROWS, cs] = acc
        return carry

    lax.fori_loop(0, shbuf.shape[0], lane_block, 0)


def _mixer_kernel(x_ref, g_ref, wz_ref, wxbc_ref, wdt_ref, wcv_ref, wcg_ref,
                  scw_ref, scb_ref, dtb_ref, aneg_ref, dskip_ref, sng_ref,
                  cw_ref, cb_ref, lng_ref, lnb_ref, wout_ref, e128_ref, e64_ref,
                  out_ref, xbuf, vbuf, shbuf, cbuf, ybuf, state):
    t = pl.program_id(1)

    @pl.when(t == 0)
    def _():
        xbuf[:, TL:TL + SSD_HALO, :] = jnp.zeros((SSD_CONV - 1, SSD_HALO, XBC_WIDTH), F32)
        vbuf[0:CONF_HALO, :] = jnp.zeros((CONF_HALO, CONF_WIDTH), F32)
        state[...] = jnp.zeros(state.shape, F32)

    x = x_ref[...]
    ub = _rmsnorm(x, g_ref[...]).astype(BF16)

    vbuf[CONF_HALO:CONF_HALO + TL, :] = _dot(ub, wcv_ref[...]) * _sigmoid(_dot(ub, wcg_ref[...]))
    half_blocks = shbuf.shape[0]
    _conv31_copies(vbuf, shbuf, TL, 0)
    _conv31_taps(vbuf, shbuf, cw_ref, cb_ref, cbuf, TL, 0)
    _conv31_copies(vbuf, shbuf, TL, half_blocks)

    pre = _dot(ub, wxbc_ref[...])
    conv = scb_ref[...] + scw_ref[SSD_CONV - 1:SSD_CONV, :] * pre
    for k in range(SSD_CONV - 1):
        xbuf[k, 0:SSD_HALO, :] = xbuf[k, TL:TL + SSD_HALO, :]
        xbuf[k, pl.ds(SSD_CONV - 1 - k, TL), :] = pre
        conv = conv + scw_ref[k:k + 1, :] * xbuf[k, 0:TL, :]
    xbc = _silu(conv)

    dt = _softplus(_dot(ub, wdt_ref[...]) + dtb_ref[...])

    lane = lax.broadcasted_iota(jnp.int32, (CHUNK, LANES), 1)
    row = lax.broadcasted_iota(jnp.int32, (CHUNK, LANES), 0)
    causal = row >= lane
    low_half = lane < SSD_HEAD_DIM
    lane_h = lax.broadcasted_iota(jnp.int32, (SSD_HEADS, CHUNK), 1)

    for c in range(TL // CHUNK):
        r0 = c * CHUNK
        xs = xbc[r0:r0 + CHUNK, :SSD_WIDTH]
        xs_bf = xs.astype(BF16)
        b_bf = xbc[r0:r0 + CHUNK, SSD_WIDTH:SSD_WIDTH + SSD_GROUPS * SSD_STATE].astype(BF16)
        c_bf = xbc[r0:r0 + CHUNK, SSD_WIDTH + SSD_GROUPS * SSD_STATE:].astype(BF16)

        dt_t = dt[r0:r0 + CHUNK, :].T[0:SSD_HEADS]
        acs_t = _cumsum_lanes(dt_t * aneg_ref[...], lane_h)
        acs2_t = acs_t * LOG2E
        rows_t = acs2_t - jnp.log2(dt_t)
        w_t = dt_t * jnp.exp(acs_t[:, CHUNK - 1:CHUNK] - acs_t)

        colb = _dot(_split3_cols(acs2_t), e128_ref[...])
        w64 = _dot(_split3_cols(w_t), e64_ref[...])

        e64_blocks = []
        y_blocks = []
        for g in range(SSD_GROUPS):
            bg = b_bf[:, g * SSD_STATE:(g + 1) * SSD_STATE]
            cg = c_bf[:, g * SSD_STATE:(g + 1) * SSD_STATE]
            scores = lax.dot_general(cg, bg, (((1,), (1,)), ((), ())),
                                     preferred_element_type=F32).astype(BF16)
            for jp in range(HEADS_PER_GROUP // 2):
                j = g * (HEADS_PER_GROUP // 2) + jp
                ms = []
                for h in (2 * j, 2 * j + 1):
                    seg = colb[:, h * LANES:(h + 1) * LANES] - rows_t[h:h + 1, :]
                    lmat = jnp.where(causal, jnp.exp2(seg), 0.0).astype(BF16)
                    ms.append(lmat * scores)
                xp = xs_bf[:, j * LANES:(j + 1) * LANES]
                zero = jnp.zeros_like(xp)
                rhs = jnp.concatenate([jnp.where(low_half, xp, zero),
                                       jnp.where(low_half, zero, xp)], axis=0)
                y_blocks.append(_dot(jnp.concatenate(ms, axis=1), rhs))
                e64_blocks.append(jnp.exp2(jnp.where(
                    low_half, colb[:, (2 * j) * LANES:(2 * j + 1) * LANES],
                    colb[:, (2 * j + 1) * LANES:(2 * j + 2) * LANES])))
        e64 = jnp.concatenate(e64_blocks, axis=1)
        y_diag = jnp.concatenate(y_blocks, axis=1)

        xd_bf = (xs * w64).astype(BF16)
        y_off = []
        for g in range(SSD_GROUPS):
            gs = slice(g * GROUP_WIDTH, (g + 1) * GROUP_WIDTH)
            bg = b_bf[:, g * SSD_STATE:(g + 1) * SSD_STATE]
            cg = c_bf[:, g * SSD_STATE:(g + 1) * SSD_STATE]
            s_old = state[g]
            y_off.append(_dot(cg, s_old.astype(BF16)) * e64[:, gs])
            new = lax.dot_general(bg, xd_bf[:, gs], (((0,), (0,)), ((), ())),
                                  preferred_element_type=F32)
            state[g] = s_old * e64[CHUNK - 1:CHUNK, gs] + new
        ybuf[r0:r0 + CHUNK, :] = y_diag + jnp.concatenate(y_off, axis=1) + xs * dskip_ref[...]

    z = _dot(ub, wz_ref[...])
    v = ybuf[...] * _silu(z)
    parts = []
    for g in range(SSD_GROUPS):
        vg = v[:, g * GROUP_WIDTH:(g + 1) * GROUP_WIDTH]
        parts.append(vg * lax.rsqrt(jnp.mean(vg * vg, axis=-1, keepdims=True) + EPS))
    y_ssd = (jnp.concatenate(parts, axis=1) * sng_ref[...]).astype(BF16)

    _conv31_taps(vbuf, shbuf, cw_ref, cb_ref, cbuf, TL, half_blocks)
    vbuf[0:CONF_HALO, :] = vbuf[TL:TL + CONF_HALO, :]

    cv = cbuf[...]
    mu = jnp.mean(cv, axis=-1, keepdims=True)
    xc = cv - mu
    ln = xc * lax.rsqrt(jnp.mean(xc * xc, axis=-1, keepdims=True) + EPS) * lng_ref[...] + lnb_ref[...]
    y_conf = _silu(ln).astype(BF16)

    out_ref[...] = (x + _dot(y_ssd, wout_ref[0:SSD_WIDTH, :])
                    + _dot(y_conf, wout_ref[SSD_WIDTH:, :]))


def _mlp_kernel(h_ref, p_ref, g1_ref, wup_ref, wdown_ref, g2_ref, wgate_ref, bgate_ref,
                wple_ref, g3_ref, gf_ref, out_ref):
    h = h_ref[...]
    u = _rmsnorm(h, g1_ref[...]).astype(BF16)
    hid = jnp.maximum(_dot(u, wup_ref[...]), 0.0)
    h = h + _dot((hid * hid).astype(BF16), wdown_ref[...])
    u = _rmsnorm(h, g2_ref[...]).astype(BF16)
    gate = _sigmoid(_dot(u, wgate_ref[...]) + bgate_ref[...])
    emb = _rmsnorm(_dot(p_ref[...].astype(BF16), wple_ref[...]), g3_ref[...])
    out_ref[...] = _rmsnorm(h + gate * emb, gf_ref[...])


def _const_spec(shape):
    nd = len(shape)
    return pl.BlockSpec(shape, lambda *_: (0,) * nd, pipeline_mode=pl.Buffered(1))


def _expand_matrix(width):
    e = np.zeros((LANES, SSD_HEADS * width), np.float32)
    for piece in range(SPLIT):
        for h in range(SSD_HEADS):
            e[piece * SSD_HEADS + h, h * width:(h + 1) * width] = 1.0
    return jnp.asarray(e, BF16)


def _mixer(x, mix_norm_g, w_in, ssd_conv_w, ssd_conv_b, dt_bias, A_log, D_skip, ssd_norm_g,
           conf_dw_w, conf_dw_b, conf_ln_g, conf_ln_b, w_out):
    b, l, d = x.shape
    o = np.cumsum([SSD_WIDTH, XBC_WIDTH, SSD_HEADS, CONF_WIDTH]).tolist()
    wb = w_in.astype(BF16)
    w_z, w_xbc, w_dt, w_cv, w_cg = (wb[:, :o[0]], wb[:, o[0]:o[1]], wb[:, o[1]:o[2]],
                                    wb[:, o[2]:o[3]], wb[:, o[3]:])
    pad_heads = lambda v: jnp.pad(v, ((0, 0), (0, LANES - SSD_HEADS)))
    row = lambda v: v.astype(F32)[None, :]
    consts = [
        row(mix_norm_g), w_z, w_xbc, pad_heads(w_dt), w_cv, w_cg,
        ssd_conv_w.astype(F32), row(ssd_conv_b), pad_heads(row(dt_bias)),
        -jnp.exp(A_log.astype(F32))[:, None],
        row(jnp.repeat(D_skip, SSD_HEAD_DIM)), row(ssd_norm_g),
        conf_dw_w.astype(F32), row(conf_dw_b), row(conf_ln_g), row(conf_ln_b), w_out.astype(BF16),
        _expand_matrix(LANES), _expand_matrix(SSD_HEAD_DIM),
    ]
    tile = pl.BlockSpec((None, TL, d), lambda bi, ti: (bi, ti, 0))
    return pl.pallas_call(
        _mixer_kernel,
        grid=(b, l // TL),
        in_specs=[tile] + [_const_spec(c.shape) for c in consts],
        out_specs=tile,
        out_shape=jax.ShapeDtypeStruct((b, l, d), F32),
        scratch_shapes=[
            pltpu.VMEM((SSD_CONV - 1, TL + SSD_HALO, XBC_WIDTH), F32),
            pltpu.VMEM((CONF_HALO + TL, CONF_WIDTH), F32),
            pltpu.VMEM((CONF_WIDTH // LANES // 2, SUBLANES - 1, TL + CONF_HALO - SUBLANES, LANES), F32),
            pltpu.VMEM((TL, CONF_WIDTH), F32),
            pltpu.VMEM((TL, SSD_WIDTH), F32),
            pltpu.VMEM((SSD_GROUPS, SSD_STATE, GROUP_WIDTH), F32),
        ],
        compiler_params=pltpu.CompilerParams(
            dimension_semantics=("arbitrary", "arbitrary"), vmem_limit_bytes=VMEM_LIMIT),
        name="mixer",
    )(x, *consts)


def _mlp(h, p, mlp_norm_g, w_up, w_down, ple_gate_norm_g, w_ple_gate, b_ple_gate, w_ple,
         ple_norm_g, final_norm_g):
    n, d = h.shape
    row = lambda v: v.astype(F32)[None, :]
    consts = [row(mlp_norm_g), w_up.astype(BF16), w_down.astype(BF16), row(ple_gate_norm_g),
              w_ple_gate.astype(BF16), row(b_ple_gate), w_ple.astype(BF16), row(ple_norm_g),
              row(final_norm_g)]
    return pl.pallas_call(
        _mlp_kernel,
        grid=(n // TM,),
        in_specs=[pl.BlockSpec((TM, d), lambda i: (i, 0)),
                  pl.BlockSpec((TM, PLE_DIM), lambda i: (i, 0))]
                 + [_const_spec(c.shape) for c in consts],
        out_specs=pl.BlockSpec((TM, d), lambda i: (i, 0)),
        out_shape=jax.ShapeDtypeStruct((n, d), F32),
        compiler_params=pltpu.CompilerParams(
            dimension_semantics=("arbitrary",), vmem_limit_bytes=VMEM_LIMIT),
        name="mlp",
    )(h, p, *consts)


def kernel(x, p, mix_norm_g, w_in, ssd_conv_w, ssd_conv_b, dt_bias, A_log, D_skip, ssd_norm_g,
           conf_dw_w, conf_dw_b, conf_ln_g, conf_ln_b, w_out, mlp_norm_g, w_up, w_down,
           ple_gate_norm_g, w_ple_gate, b_ple_gate, w_ple, ple_norm_g, final_norm_g):
    b, l, d = x.shape
    assert (d, l % TL, (b * l) % TM, mix_norm_g.shape[0]) == (D_MODEL, 0, 0, 1)
    h = _mixer(x, mix_norm_g[0], w_in[0], ssd_conv_w[0], ssd_conv_b[0], dt_bias[0], A_log[0],
               D_skip[0], ssd_norm_g[0], conf_dw_w[0], conf_dw_b[0], conf_ln_g[0], conf_ln_b[0],
               w_out[0])
    out = _mlp(h.reshape(b * l, d), p[0].reshape(b * l, PLE_DIM), mlp_norm_g[0], w_up[0],
               w_down[0], ple_gate_norm_g[0], w_ple_gate[0], b_ple_gate[0], w_ple[0],
               ple_norm_g[0], final_norm_g)
    return out.reshape(b, l, d)
```

```python
import math

import numpy as np
import jax
import jax.numpy as jnp
from jax import lax
from jax.experimental import pallas as pl
from jax.experimental.pallas import tpu as pltpu

D_MODEL = 1024
SSD_WIDTH = 1024
SSD_HEAD_DIM = 64
SSD_HEADS = 16
SSD_GROUPS = 2
SSD_STATE = 128
SSD_CONV = 4
CHUNK = 128
CONF_WIDTH = 1024
CONF_KERNEL = 31
D_FF = 4096
PLE_DIM = 256
EPS = 1e-6
XBC_WIDTH = SSD_WIDTH + 2 * SSD_GROUPS * SSD_STATE
GROUP_WIDTH = SSD_WIDTH // SSD_GROUPS
HEADS_PER_GROUP = SSD_HEADS // SSD_GROUPS

LANES = 128
SUBLANES = 8
SPLIT = 3
SSD_HALO = SUBLANES
CONF_HALO = 32
CONV_ROWS = 32

TL = 512
TM = 1024
VMEM_LIMIT = 56 * 1024 * 1024
LOG2E = math.log2(math.e)

F32 = jnp.float32
BF16 = jnp.bfloat16


def _sigmoid(x):
    return 1.0 / (1.0 + jnp.exp(-x))


def _silu(x):
    h = 0.5 * x
    return h + h * jnp.tanh(h)


def _softplus(x):
    return jnp.maximum(x, 0.0) + jnp.log1p(jnp.exp(-jnp.abs(x)))


def _rmsnorm(x, g):
    return x * lax.rsqrt(jnp.mean(x * x, axis=-1, keepdims=True) + EPS) * g


def _dot(a, b):
    return jnp.dot(a, b, preferred_element_type=F32)


def _split3_cols(v_t):
    hi = v_t.astype(BF16).astype(F32)
    r1 = v_t - hi
    mid = r1.astype(BF16).astype(F32)
    lo = r1 - mid
    pad = jnp.zeros((LANES - SPLIT * SSD_HEADS, v_t.shape[1]), F32)
    return jnp.concatenate([hi, mid, lo, pad], axis=0).T.astype(BF16)


def _cumsum_lanes(a, lane):
    s = 1
    while s < a.shape[1]:
        a = a + jnp.where(lane >= s, pltpu.roll(a, s, axis=1), 0.0)
        s *= 2
    return a


def _conv31_copies(vbuf, shbuf, rows, first_blk):
    span = rows + CONF_HALO - SUBLANES
    for j in range(shbuf.shape[0]):
        cs = slice((first_blk + j) * LANES, (first_blk + j + 1) * LANES)
        for r in range(1, SUBLANES):
            shbuf[j, r - 1] = vbuf[pl.ds(r, span), cs]


def _conv31_taps(vbuf, shbuf, cw_ref, cb_ref, cbuf, rows, first_blk):
    first = CONF_HALO - (CONF_KERNEL - 1)

    def lane_block(j, carry):
        cs = pl.ds(pl.multiple_of((first_blk + j) * LANES, LANES), LANES)
        for rb in range(0, rows, CONV_ROWS):
            acc = jnp.broadcast_to(cb_ref[0:1, cs], (CONV_ROWS, LANES))
            for k in range(CONF_KERNEL):
                q, r = divmod(first + k, SUBLANES)
                lo = q * SUBLANES + rb
                win = shbuf[j, r - 1, pl.ds(lo, CONV_ROWS), :] if r else vbuf[pl.ds(lo, CONV_ROWS), cs]
                acc = acc + win * cw_ref[k:k + 1, cs]
            cbuf[rb:rb + CONV_ROWS, cs] = acc
        return carry

    lax.fori_loop(0, shbuf.shape[0], lane_block, 0)


def _mixer_kernel(x_ref, g_ref, wz_ref, wxbc_ref, wdt_ref, wcv_ref, wcg_ref,
                  scw_ref, scb_ref, dtb_ref, aneg_ref, dskip_ref, sng_ref,
                  cw_ref, cb_ref, lng_ref, lnb_ref, wout_ref, e128_ref, e64_ref,
                  out_ref, xbuf, vbuf, shbuf, cbuf, ybuf, state):
    t = pl.program_id(1)

    @pl.when(t == 0)
    def _():
        xbuf[:, TL:TL + SSD_HALO, :] = jnp.zeros((SSD_CONV - 1, SSD_HALO, XBC_WIDTH), F32)
        vbuf[0:CONF_HALO, :] = jnp.zeros((CONF_HALO, CONF_WIDTH), F32)
        state[...] = jnp.zeros(state.shape, F32)

    x = x_ref[...]
    ub = _rmsnorm(x, g_ref[...]).astype(BF16)

    vbuf[CONF_HALO:CONF_HALO + TL, :] = _dot(ub, wcv_ref[...]) * _sigmoid(_dot(ub, wcg_ref[...]))
    half_blocks = shbuf.shape[0]
    _conv31_copies(vbuf, shbuf, TL, 0)
    _conv31_taps(vbuf, shbuf, cw_ref, cb_ref, cbuf, TL, 0)
    _conv31_copies(vbuf, shbuf, TL, half_blocks)

    pre = _dot(ub, wxbc_ref[...])
    conv = scb_ref[...] + scw_ref[SSD_CONV - 1:SSD_CONV, :] * pre
    for k in range(SSD_CONV - 1):
        xbuf[k, 0:SSD_HALO, :] = xbuf[k, TL:TL + SSD_HALO, :]
        xbuf[k, pl.ds(SSD_CONV - 1 - k, TL), :] = pre
        conv = conv + scw_ref[k:k + 1, :] * xbuf[k, 0:TL, :]
    xbc = _silu(conv)

    dt = _softplus(_dot(ub, wdt_ref[...]) + dtb_ref[...])

    lane = lax.broadcasted_iota(jnp.int32, (CHUNK, LANES), 1)
    row = lax.broadcasted_iota(jnp.int32, (CHUNK, LANES), 0)
    causal = row >= lane
    low_half = lane < SSD_HEAD_DIM
    lane_h = lax.broadcasted_iota(jnp.int32, (SSD_HEADS, CHUNK), 1)

    for c in range(TL // CHUNK):
        r0 = c * CHUNK
        xs = xbc[r0:r0 + CHUNK, :SSD_WIDTH]
        xs_bf = xs.astype(BF16)
        b_bf = xbc[r0:r0 + CHUNK, SSD_WIDTH:SSD_WIDTH + SSD_GROUPS * SSD_STATE].astype(BF16)
        c_bf = xbc[r0:r0 + CHUNK, SSD_WIDTH + SSD_GROUPS * SSD_STATE:].astype(BF16)

        dt_t = dt[r0:r0 + CHUNK, :].T[0:SSD_HEADS]
        acs_t = _cumsum_lanes(dt_t * aneg_ref[...], lane_h)
        acs2_t = acs_t * LOG2E
        rows_t = acs2_t - jnp.log2(dt_t)
        w_t = dt_t * jnp.exp(acs_t[:, CHUNK - 1:CHUNK] - acs_t)

        colb = _dot(_split3_cols(acs2_t), e128_ref[...])
        w64 = _dot(_split3_cols(w_t), e64_ref[...])

        e64_blocks = []
        y_blocks = []
        for g in range(SSD_GROUPS):
            bg = b_bf[:, g * SSD_STATE:(g + 1) * SSD_STATE]
            cg = c_bf[:, g * SSD_STATE:(g + 1) * SSD_STATE]
            scores = lax.dot_general(cg, bg, (((1,), (1,)), ((), ())),
                                     preferred_element_type=F32).astype(BF16)
            for jp in range(HEADS_PER_GROUP // 2):
                j = g * (HEADS_PER_GROUP // 2) + jp
                ms = []
                for h in (2 * j, 2 * j + 1):
                    seg = colb[:, h * LANES:(h + 1) * LANES] - rows_t[h:h + 1, :]
                    lmat = jnp.where(causal, jnp.exp2(seg), 0.0).astype(BF16)
                    ms.append(lmat * scores)
                xp = xs_bf[:, j * LANES:(j + 1) * LANES]
                zero = jnp.zeros_like(xp)
                rhs = jnp.concatenate([jnp.where(low_half, xp, zero),
                                       jnp.where(low_half, zero, xp)], axis=0)
                y_blocks.append(_dot(jnp.concatenate(ms, axis=1), rhs))
                e64_blocks.append(jnp.exp2(jnp.where(
                    low_half, colb[:, (2 * j) * LANES:(2 * j + 1) * LANES],
                    colb[:, (2 * j + 1) * LANES:(2 * j + 2) * LANES])))
        e64 = jnp.concatenate(e64_blocks, axis=1)
        y_diag = jnp.concatenate(y_blocks, axis=1)

        xd_bf = (xs * w64).astype(BF16)
        y_off = []
        for g in range(SSD_GROUPS):
            gs = slice(g * GROUP_WIDTH, (g + 1) * GROUP_WIDTH)
            bg = b_bf[:, g * SSD_STATE:(g + 1) * SSD_STATE]
            cg = c_bf[:, g * SSD_STATE:(g + 1) * SSD_STATE]
            s_old = state[g]
            y_off.append(_dot(cg, s_old.astype(BF16)) * e64[:, gs])
            new = lax.dot_general(bg, xd_bf[:, gs], (((0,), (0,)), ((), ())),
                                  preferred_element_type=F32)
            state[g] = s_old * e64[CHUNK - 1:CHUNK, gs] + new
        ybuf[r0:r0 + CHUNK, :] = y_diag + jnp.concatenate(y_off, axis=1) + xs * dskip_ref[...]

    z = _dot(ub, wz_ref[...])
    v = ybuf[...] * _silu(z)
    parts = []
    for g in range(SSD_GROUPS):
        vg = v[:, g * GROUP_WIDTH:(g + 1) * GROUP_WIDTH]
        parts.append(vg * lax.rsqrt(jnp.mean(vg * vg, axis=-1, keepdims=True) + EPS))
    y_ssd = (jnp.concatenate(parts, axis=1) * sng_ref[...]).astype(BF16)

    _conv31_taps(vbuf, shbuf, cw_ref, cb_ref, cbuf, TL, half_blocks)
    vbuf[0:CONF_HALO, :] = vbuf[TL:TL + CONF_HALO, :]

    cv = cbuf[...]
    mu = jnp.mean(cv, axis=-1, keepdims=True)
    xc = cv - mu
    ln = xc * lax.rsqrt(jnp.mean(xc * xc, axis=-1, keepdims=True) + EPS) * lng_ref[...] + lnb_ref[...]
    y_conf = _silu(ln).astype(BF16)

    out_ref[...] = (x + _dot(y_ssd, wout_ref[0:SSD_WIDTH, :])
                    + _dot(y_conf, wout_ref[SSD_WIDTH:, :]))


def _zero_after(x):
    bits = lax.shift_right_logical(lax.shift_right_logical(pltpu.bitcast(x, jnp.int32), 16), 16)
    return bits.astype(F32)


def _mlp_kernel(h_ref, p_ref, g1_ref, wup_ref, wdown_ref, g2_ref, wgate_ref, bgate_ref,
                wple_ref, g3_ref, gf_ref, out_ref):
    h = h_ref[...]
    u = _rmsnorm(h, g1_ref[...]).astype(BF16)
    pre = _dot(u, wup_ref[...])
    pe = p_ref[...] + _zero_after(pre[:, :PLE_DIM])
    emb = _rmsnorm(_dot(pe.astype(BF16), wple_ref[...]), g3_ref[...])
    hid = jnp.maximum(pre, 0.0)
    hid = jnp.concatenate([hid[:, :LANES] + _zero_after(emb[:, :LANES]), hid[:, LANES:]], axis=1)
    h = h + _dot((hid * hid).astype(BF16), wdown_ref[...])
    u = _rmsnorm(h, g2_ref[...]).astype(BF16)
    gate = _sigmoid(_dot(u, wgate_ref[...]) + bgate_ref[...])
    out_ref[...] = _rmsnorm(h + gate * emb, gf_ref[...])


def _const_spec(shape):
    nd = len(shape)
    return pl.BlockSpec(shape, lambda *_: (0,) * nd, pipeline_mode=pl.Buffered(1))


def _expand_matrix(width):
    e = np.zeros((LANES, SSD_HEADS * width), np.float32)
    for piece in range(SPLIT):
        for h in range(SSD_HEADS):
            e[piece * SSD_HEADS + h, h * width:(h + 1) * width] = 1.0
    return jnp.asarray(e, BF16)


def _mixer(x, mix_norm_g, w_in, ssd_conv_w, ssd_conv_b, dt_bias, A_log, D_skip, ssd_norm_g,
           conf_dw_w, conf_dw_b, conf_ln_g, conf_ln_b, w_out):
    b, l, d = x.shape
    o = np.cumsum([SSD_WIDTH, XBC_WIDTH, SSD_HEADS, CONF_WIDTH]).tolist()
    wb = w_in.astype(BF16)
    w_z, w_xbc, w_dt, w_cv, w_cg = (wb[:, :o[0]], wb[:, o[0]:o[1]], wb[:, o[1]:o[2]],
                                    wb[:, o[2]:o[3]], wb[:, o[3]:])
    pad_heads = lambda v: jnp.pad(v, ((0, 0), (0, LANES - SSD_HEADS)))
    row = lambda v: v.astype(F32)[None, :]
    consts = [
        row(mix_norm_g), w_z, w_xbc, pad_heads(w_dt), w_cv, w_cg,
        ssd_conv_w.astype(F32), row(ssd_conv_b), pad_heads(row(dt_bias)),
        -jnp.exp(A_log.astype(F32))[:, None],
        row(jnp.repeat(D_skip, SSD_HEAD_DIM)), row(ssd_norm_g),
        conf_dw_w.astype(F32), row(conf_dw_b), row(conf_ln_g), row(conf_ln_b), w_out.astype(BF16),
        _expand_matrix(LANES), _expand_matrix(SSD_HEAD_DIM),
    ]
    tile = pl.BlockSpec((None, TL, d), lambda bi, ti: (bi, ti, 0))
    return pl.pallas_call(
        _mixer_kernel,
        grid=(b, l // TL),
        in_specs=[tile] + [_const_spec(c.shape) for c in consts],
        out_specs=tile,
        out_shape=jax.ShapeDtypeStruct((b, l, d), F32),
        scratch_shapes=[
            pltpu.VMEM((SSD_CONV - 1, TL + SSD_HALO, XBC_WIDTH), F32),
            pltpu.VMEM((CONF_HALO + TL, CONF_WIDTH), F32),
            pltpu.VMEM((CONF_WIDTH // LANES // 2, SUBLANES - 1, TL + CONF_HALO - SUBLANES, LANES), F32),
            pltpu.VMEM((TL, CONF_WIDTH), F32),
            pltpu.VMEM((TL, SSD_WIDTH), F32),
            pltpu.VMEM((SSD_GROUPS, SSD_STATE, GROUP_WIDTH), F32),
        ],
        compiler_params=pltpu.CompilerParams(
            dimension_semantics=("arbitrary", "arbitrary"), vmem_limit_bytes=VMEM_LIMIT),
        name="mixer",
    )(x, *consts)


def _mlp(h, p, mlp_norm_g, w_up, w_down, ple_gate_norm_g, w_ple_gate, b_ple_gate, w_ple,
         ple_norm_g, final_norm_g):
    n, d = h.shape
    row = lambda v: v.astype(F32)[None, :]
    consts = [row(mlp_norm_g), w_up.astype(BF16), w_down.astype(BF16), row(ple_gate_norm_g),
              w_ple_gate.astype(BF16), row(b_ple_gate), w_ple.astype(BF16), row(ple_norm_g),
              row(final_norm_g)]
    return pl.pallas_call(
        _mlp_kernel,
        grid=(n // TM,),
        in_specs=[pl.BlockSpec((TM, d), lambda i: (i, 0)),
                  pl.BlockSpec((TM, PLE_DIM), lambda i: (i, 0))]
                 + [_const_spec(c.shape) for c in consts],
        out_specs=pl.BlockSpec((TM, d), lambda i: (i, 0)),
        out_shape=jax.ShapeDtypeStruct((n, d), F32),
        compiler_params=pltpu.CompilerParams(
            dimension_semantics=("arbitrary",), vmem_limit_bytes=VMEM_LIMIT),
        name="mlp",
    )(h, p, *consts)


def kernel(x, p, mix_norm_g, w_in, ssd_conv_w, ssd_conv_b, dt_bias, A_log, D_skip, ssd_norm_g,
           conf_dw_w, conf_dw_b, conf_ln_g, conf_ln_b, w_out, mlp_norm_g, w_up, w_down,
           ple_gate_norm_g, w_ple_gate, b_ple_gate, w_ple, ple_norm_g, final_norm_g):
    b, l, d = x.shape
    assert (d, l % TL, (b * l) % TM, mix_norm_g.shape[0]) == (D_MODEL, 0, 0, 1)
    h = _mixer(x, mix_norm_g[0], w_in[0], ssd_conv_w[0], ssd_conv_b[0], dt_bias[0], A_log[0],
               D_skip[0], ssd_norm_g[0], conf_dw_w[0], conf_dw_b[0], conf_ln_g[0], conf_ln_b[0],
               w_out[0])
    out = _mlp(h.reshape(b * l, d), p[0].reshape(b * l, PLE_DIM), mlp_norm_g[0], w_up[0],
               w_down[0], ple_gate_norm_g[0], w_ple_gate[0], b_ple_gate[0], w_ple[0],
               ple_norm_g[0], final_norm_g)
    return out.reshape(b, l, d)
```

```python
import math

import numpy as np
import jax
import jax.numpy as jnp
from jax import lax
from jax.experimental import pallas as pl
from jax.experimental.pallas import tpu as pltpu

D_MODEL = 1024
SSD_WIDTH = 1024
SSD_HEAD_DIM = 64
SSD_HEADS = 16
SSD_GROUPS = 2
SSD_STATE = 128
SSD_CONV = 4
CHUNK = 128
CONF_WIDTH = 1024
CONF_KERNEL = 31
D_FF = 4096
PLE_DIM = 256
EPS = 1e-6
XBC_WIDTH = SSD_WIDTH + 2 * SSD_GROUPS * SSD_STATE
GROUP_WIDTH = SSD_WIDTH // SSD_GROUPS
HEADS_PER_GROUP = SSD_HEADS // SSD_GROUPS

LANES = 128
SUBLANES = 8
SPLIT = 3
SSD_HALO = SUBLANES
CONF_HALO = 32
CONV_ROWS = 32
Z_BEFORE_CHUNK = 2

TL = 512
TM = 1024
VMEM_LIMIT = 56 * 1024 * 1024
LOG2E = math.log2(math.e)

F32 = jnp.float32
BF16 = jnp.bfloat16


def _sigmoid(x):
    return 1.0 / (1.0 + jnp.exp(-x))


def _silu(x):
    h = 0.5 * x
    return h + h * jnp.tanh(h)


def _softplus(x):
    return jnp.maximum(x, 0.0) + jnp.log1p(jnp.exp(-jnp.abs(x)))


def _rmsnorm(x, g):
    return x * lax.rsqrt(jnp.mean(x * x, axis=-1, keepdims=True) + EPS) * g


def _dot(a, b):
    return jnp.dot(a, b, preferred_element_type=F32)


def _split3_cols(v_t):
    hi = v_t.astype(BF16).astype(F32)
    r1 = v_t - hi
    mid = r1.astype(BF16).astype(F32)
    lo = r1 - mid
    pad = jnp.zeros((LANES - SPLIT * SSD_HEADS, v_t.shape[1]), F32)
    return jnp.concatenate([hi, mid, lo, pad], axis=0).T.astype(BF16)


def _cumsum_lanes(a, lane):
    s = 1
    while s < a.shape[1]:
        a = a + jnp.where(lane >= s, pltpu.roll(a, s, axis=1), 0.0)
        s *= 2
    return a


def _zero_after(x):
    bits = lax.shift_right_logical(lax.shift_right_logical(pltpu.bitcast(x, jnp.int32), 16), 16)
    return bits.astype(F32)


def _conv31_copies(vbuf, shbuf, rows, first_blk):
    span = rows + CONF_HALO - SUBLANES
    for j in range(shbuf.shape[0]):
        cs = slice((first_blk + j) * LANES, (first_blk + j + 1) * LANES)
        for r in range(1, SUBLANES):
            shbuf[j, r - 1] = vbuf[pl.ds(r, span), cs]


def _conv31_taps(vbuf, shbuf, cw_ref, cb_ref, cbuf, rows, first_blk):
    first = CONF_HALO - (CONF_KERNEL - 1)

    def lane_block(j, carry):
        cs = pl.ds(pl.multiple_of((first_blk + j) * LANES, LANES), LANES)
        for rb in range(0, rows, CONV_ROWS):
            acc = jnp.broadcast_to(cb_ref[0:1, cs], (CONV_ROWS, LANES))
            for k in range(CONF_KERNEL):
                q, r = divmod(first + k, SUBLANES)
                lo = q * SUBLANES + rb
                win = shbuf[j, r - 1, pl.ds(lo, CONV_ROWS), :] if r else vbuf[pl.ds(lo, CONV_ROWS), cs]
                acc = acc + win * cw_ref[k:k + 1, cs]
            cbuf[rb:rb + CONV_ROWS, cs] = acc
        return carry

    lax.fori_loop(0, shbuf.shape[0], lane_block, 0)


def _mixer_kernel(x_ref, g_ref, wz_ref, wxbc_ref, wdt_ref, wcv_ref, wcg_ref,
                  scw_ref, scb_ref, dtb_ref, aneg_ref, dskip_ref, sng_ref,
                  cw_ref, cb_ref, lng_ref, lnb_ref, wout_ref, e128_ref, e64_ref,
                  out_ref, xbuf, vbuf, shbuf, cbuf, ybuf, state):
    t = pl.program_id(1)

    @pl.when(t == 0)
    def _():
        xbuf[:, TL:TL + SSD_HALO, :] = jnp.zeros((SSD_CONV - 1, SSD_HALO, XBC_WIDTH), F32)
        vbuf[0:CONF_HALO, :] = jnp.zeros((CONF_HALO, CONF_WIDTH), F32)
        state[...] = jnp.zeros(state.shape, F32)

    x = x_ref[...]
    ub = _rmsnorm(x, g_ref[...]).astype(BF16)

    vbuf[CONF_HALO:CONF_HALO + TL, :] = _dot(ub, wcv_ref[...]) * _sigmoid(_dot(ub, wcg_ref[...]))
    half_blocks = shbuf.shape[0]
    _conv31_copies(vbuf, shbuf, TL, 0)
    _conv31_taps(vbuf, shbuf, cw_ref, cb_ref, cbuf, TL, 0)
    _conv31_copies(vbuf, shbuf, TL, half_blocks)

    pre = _dot(ub, wxbc_ref[...])
    conv = scb_ref[...] + scw_ref[SSD_CONV - 1:SSD_CONV, :] * pre
    for k in range(SSD_CONV - 1):
        xbuf[k, 0:SSD_HALO, :] = xbuf[k, TL:TL + SSD_HALO, :]
        xbuf[k, pl.ds(SSD_CONV - 1 - k, TL), :] = pre
        conv = conv + scw_ref[k:k + 1, :] * xbuf[k, 0:TL, :]
    xbc = _silu(conv)

    dt = _softplus(_dot(ub, wdt_ref[...]) + dtb_ref[...])

    lane = lax.broadcasted_iota(jnp.int32, (CHUNK, LANES), 1)
    row = lax.broadcasted_iota(jnp.int32, (CHUNK, LANES), 0)
    causal = row >= lane
    low_half = lane < SSD_HEAD_DIM
    lane_h = lax.broadcasted_iota(jnp.int32, (SSD_HEADS, CHUNK), 1)

    z = _dot(ub, wz_ref[...])

    for c in range(TL // CHUNK):
        r0 = c * CHUNK
        xs = xbc[r0:r0 + CHUNK, :SSD_WIDTH]
        xs_bf = xs.astype(BF16)
        b_bf = xbc[r0:r0 + CHUNK, SSD_WIDTH:SSD_WIDTH + SSD_GROUPS * SSD_STATE].astype(BF16)
        c_bf = xbc[r0:r0 + CHUNK, SSD_WIDTH + SSD_GROUPS * SSD_STATE:].astype(BF16)

        dt_t = dt[r0:r0 + CHUNK, :].T[0:SSD_HEADS]
        acs_t = _cumsum_lanes(dt_t * aneg_ref[...], lane_h)
        acs2_t = acs_t * LOG2E
        rows_t = acs2_t - jnp.log2(dt_t)
        w_t = dt_t * jnp.exp(acs_t[:, CHUNK - 1:CHUNK] - acs_t)

        colb = _dot(_split3_cols(acs2_t), e128_ref[...])
        w64 = _dot(_split3_cols(w_t), e64_ref[...])

        e64_blocks = []
        y_blocks = []
        for g in range(SSD_GROUPS):
            bg = b_bf[:, g * SSD_STATE:(g + 1) * SSD_STATE]
            cg = c_bf[:, g * SSD_STATE:(g + 1) * SSD_STATE]
            scores = lax.dot_general(cg, bg, (((1,), (1,)), ((), ())),
                                     preferred_element_type=F32).astype(BF16)
            for jp in range(HEADS_PER_GROUP // 2):
                j = g * (HEADS_PER_GROUP // 2) + jp
                ms = []
                for h in (2 * j, 2 * j + 1):
                    seg = colb[:, h * LANES:(h + 1) * LANES] - rows_t[h:h + 1, :]
                    lmat = jnp.where(causal, jnp.exp2(seg), 0.0).astype(BF16)
                    ms.append(lmat * scores)
                xp = xs_bf[:, j * LANES:(j + 1) * LANES]
                zero = jnp.zeros_like(xp)
                rhs = jnp.concatenate([jnp.where(low_half, xp, zero),
                                       jnp.where(low_half, zero, xp)], axis=0)
                y_blocks.append(_dot(jnp.concatenate(ms, axis=1), rhs))
                e64_blocks.append(jnp.exp2(jnp.where(
                    low_half, colb[:, (2 * j) * LANES:(2 * j + 1) * LANES],
                    colb[:, (2 * j + 1) * LANES:(2 * j + 2) * LANES])))
        e64 = jnp.concatenate(e64_blocks, axis=1)
        if c == Z_BEFORE_CHUNK:
            y_blocks[0] = (y_blocks[0] + _zero_after(z[TL - CHUNK:, :LANES])
                           + _zero_after(z[TL - CHUNK:, -LANES:]))
        y_diag = jnp.concatenate(y_blocks, axis=1)

        xd_bf = (xs * w64).astype(BF16)
        y_off = []
        for g in range(SSD_GROUPS):
            gs = slice(g * GROUP_WIDTH, (g + 1) * GROUP_WIDTH)
            bg = b_bf[:, g * SSD_STATE:(g + 1) * SSD_STATE]
            cg = c_bf[:, g * SSD_STATE:(g + 1) * SSD_STATE]
            s_old = state[g]
            y_off.append(_dot(cg, s_old.astype(BF16)) * e64[:, gs])
            new = lax.dot_general(bg, xd_bf[:, gs], (((0,), (0,)), ((), ())),
                                  preferred_element_type=F32)
            state[g] = s_old * e64[CHUNK - 1:CHUNK, gs] + new
        ybuf[r0:r0 + CHUNK, :] = y_diag + jnp.concatenate(y_off, axis=1) + xs * dskip_ref[...]

    v = ybuf[...] * _silu(z)
    parts = []
    for g in range(SSD_GROUPS):
        vg = v[:, g * GROUP_WIDTH:(g + 1) * GROUP_WIDTH]
        parts.append(vg * lax.rsqrt(jnp.mean(vg * vg, axis=-1, keepdims=True) + EPS))
    y_ssd = (jnp.concatenate(parts, axis=1) * sng_ref[...]).astype(BF16)

    _conv31_taps(vbuf, shbuf, cw_ref, cb_ref, cbuf, TL, half_blocks)
    vbuf[0:CONF_HALO, :] = vbuf[TL:TL + CONF_HALO, :]

    cv = cbuf[...]
    mu = jnp.mean(cv, axis=-1, keepdims=True)
    xc = cv - mu
    ln = xc * lax.rsqrt(jnp.mean(xc * xc, axis=-1, keepdims=True) + EPS) * lng_ref[...] + lnb_ref[...]
    y_conf = _silu(ln).astype(BF16)

    out_ref[...] = (x + _dot(y_ssd, wout_ref[0:SSD_WIDTH, :])
                    + _dot(y_conf, wout_ref[SSD_WIDTH:, :]))


def _mlp_kernel(h_ref, p_ref, g1_ref, wup_ref, wdown_ref, g2_ref, wgate_ref, bgate_ref,
                wple_ref, g3_ref, gf_ref, out_ref):
    h = h_ref[...]
    u = _rmsnorm(h, g1_ref[...]).astype(BF16)
    pre = _dot(u, wup_ref[...])
    pe = p_ref[...] + _zero_after(pre[:, :PLE_DIM])
    emb = _rmsnorm(_dot(pe.astype(BF16), wple_ref[...]), g3_ref[...])
    hid = jnp.maximum(pre, 0.0)
    hid = jnp.concatenate([hid[:, :LANES] + _zero_after(emb[:, :LANES]), hid[:, LANES:]], axis=1)
    h = h + _dot((hid * hid).astype(BF16), wdown_ref[...])
    u = _rmsnorm(h, g2_ref[...]).astype(BF16)
    gate = _sigmoid(_dot(u, wgate_ref[...]) + bgate_ref[...])
    out_ref[...] = _rmsnorm(h + gate * emb, gf_ref[...])


def _const_spec(shape):
    nd = len(shape)
    return pl.BlockSpec(shape, lambda *_: (0,) * nd, pipeline_mode=pl.Buffered(1))


def _expand_matrix(width):
    e = np.zeros((LANES, SSD_HEADS * width), np.float32)
    for piece in range(SPLIT):
        for h in range(SSD_HEADS):
            e[piece * SSD_HEADS + h, h * width:(h + 1) * width] = 1.0
    return jnp.asarray(e, BF16)


def _mixer(x, mix_norm_g, w_in, ssd_conv_w, ssd_conv_b, dt_bias, A_log, D_skip, ssd_norm_g,
           conf_dw_w, conf_dw_b, conf_ln_g, conf_ln_b, w_out):
    b, l, d = x.shape
    o = np.cumsum([SSD_WIDTH, XBC_WIDTH, SSD_HEADS, CONF_WIDTH]).tolist()
    wb = w_in.astype(BF16)
    w_z, w_xbc, w_dt, w_cv, w_cg = (wb[:, :o[0]], wb[:, o[0]:o[1]], wb[:, o[1]:o[2]],
                                    wb[:, o[2]:o[3]], wb[:, o[3]:])
    pad_heads = lambda v: jnp.pad(v, ((0, 0), (0, LANES - SSD_HEADS)))
    row = lambda v: v.astype(F32)[None, :]
    consts = [
        row(mix_norm_g), w_z, w_xbc, pad_heads(w_dt), w_cv, w_cg,
        ssd_conv_w.astype(F32), row(ssd_conv_b), pad_heads(row(dt_bias)),
        -jnp.exp(A_log.astype(F32))[:, None],
        row(jnp.repeat(D_skip, SSD_HEAD_DIM)), row(ssd_norm_g),
        conf_dw_w.astype(F32), row(conf_dw_b), row(conf_ln_g), row(conf_ln_b), w_out.astype(BF16),
        _expand_matrix(LANES), _expand_matrix(SSD_HEAD_DIM),
    ]
    tile = pl.BlockSpec((None, TL, d), lambda bi, ti: (bi, ti, 0))
    return pl.pallas_call(
        _mixer_kernel,
        grid=(b, l // TL),
        in_specs=[tile] + [_const_spec(c.shape) for c in consts],
        out_specs=tile,
        out_shape=jax.ShapeDtypeStruct((b, l, d), F32),
        scratch_shapes=[
            pltpu.VMEM((SSD_CONV - 1, TL + SSD_HALO, XBC_WIDTH), F32),
            pltpu.VMEM((CONF_HALO + TL, CONF_WIDTH), F32),
            pltpu.VMEM((CONF_WIDTH // LANES // 2, SUBLANES - 1, TL + CONF_HALO - SUBLANES, LANES), F32),
            pltpu.VMEM((TL, CONF_WIDTH), F32),
            pltpu.VMEM((TL, SSD_WIDTH), F32),
            pltpu.VMEM((SSD_GROUPS, SSD_STATE, GROUP_WIDTH), F32),
        ],
        compiler_params=pltpu.CompilerParams(
            dimension_semantics=("arbitrary", "arbitrary"), vmem_limit_bytes=VMEM_LIMIT),
        name="mixer",
    )(x, *consts)


def _mlp(h, p, mlp_norm_g, w_up, w_down, ple_gate_norm_g, w_ple_gate, b_ple_gate, w_ple,
         ple_norm_g, final_norm_g):
    n, d = h.shape
    row = lambda v: v.astype(F32)[None, :]
    consts = [row(mlp_norm_g), w_up.astype(BF16), w_down.astype(BF16), row(ple_gate_norm_g),
              w_ple_gate.astype(BF16), row(b_ple_gate), w_ple.astype(BF16), row(ple_norm_g),
              row(final_norm_g)]
    return pl.pallas_call(
        _mlp_kernel,
        grid=(n // TM,),
        in_specs=[pl.BlockSpec((TM, d), lambda i: (i, 0)),
                  pl.BlockSpec((TM, PLE_DIM), lambda i: (i, 0))]
                 + [_const_spec(c.shape) for c in consts],
        out_specs=pl.BlockSpec((TM, d), lambda i: (i, 0)),
        out_shape=jax.ShapeDtypeStruct((n, d), F32),
        compiler_params=pltpu.CompilerParams(
            dimension_semantics=("arbitrary",), vmem_limit_bytes=VMEM_LIMIT),
        name="mlp",
    )(h, p, *consts)


def kernel(x, p, mix_norm_g, w_in, ssd_conv_w, ssd_conv_b, dt_bias, A_log, D_skip, ssd_norm_g,
           conf_dw_w, conf_dw_b, conf_ln_g, conf_ln_b, w_out, mlp_norm_g, w_up, w_down,
           ple_gate_norm_g, w_ple_gate, b_ple_gate, w_ple, ple_norm_g, final_norm_g):
    b, l, d = x.shape
    assert (d, l % TL, (b * l) % TM, mix_norm_g.shape[0]) == (D_MODEL, 0, 0, 1)
    h = _mixer(x, mix_norm_g[0], w_in[0], ssd_conv_w[0], ssd_conv_b[0], dt_bias[0], A_log[0],
               D_skip[0], ssd_norm_g[0], conf_dw_w[0], conf_dw_b[0], conf_ln_g[0], conf_ln_b[0],
               w_out[0])
    out = _mlp(h.reshape(b * l, d), p[0].reshape(b * l, PLE_DIM), mlp_norm_g[0], w_up[0],
               w_down[0], ple_gate_norm_g[0], w_ple_gate[0], b_ple_gate[0], w_ple[0],
               ple_norm_g[0], final_norm_g)
    return out.reshape(b, l, d)
```

```python
import math

import numpy as np
import jax
import jax.numpy as jnp
from jax import lax
from jax.experimental import pallas as pl
from jax.experimental.pallas import tpu as pltpu

D_MODEL = 1024
SSD_WIDTH = 1024
SSD_HEAD_DIM = 64
SSD_HEADS = 16
SSD_GROUPS = 2
SSD_STATE = 128
SSD_CONV = 4
CHUNK = 128
CONF_WIDTH = 1024
CONF_KERNEL = 31
D_FF = 4096
PLE_DIM = 256
EPS = 1e-6
XBC_WIDTH = SSD_WIDTH + 2 * SSD_GROUPS * SSD_STATE
GROUP_WIDTH = SSD_WIDTH // SSD_GROUPS
HEADS_PER_GROUP = SSD_HEADS // SSD_GROUPS

LANES = 128
SUBLANES = 8
SPLIT = 3
SSD_HALO = SUBLANES
CONF_HALO = 32
CONV_ROWS = 32
Z_BEFORE_CHUNK = 1

TL = 512
TM = 1024
VMEM_LIMIT = 56 * 1024 * 1024
LOG2E = math.log2(math.e)

F32 = jnp.float32
BF16 = jnp.bfloat16


def _sigmoid(x):
    return 1.0 / (1.0 + jnp.exp(-x))


def _silu(x):
    h = 0.5 * x
    return h + h * jnp.tanh(h)


def _softplus(x):
    return jnp.maximum(x, 0.0) + jnp.log1p(jnp.exp(-jnp.abs(x)))


def _rmsnorm(x, g):
    return x * lax.rsqrt(jnp.mean(x * x, axis=-1, keepdims=True) + EPS) * g


def _dot(a, b):
    return jnp.dot(a, b, preferred_element_type=F32)


def _split3_cols(v_t):
    hi = v_t.astype(BF16).astype(F32)
    r1 = v_t - hi
    mid = r1.astype(BF16).astype(F32)
    lo = r1 - mid
    pad = jnp.zeros((LANES - SPLIT * SSD_HEADS, v_t.shape[1]), F32)
    return jnp.concatenate([hi, mid, lo, pad], axis=0).T.astype(BF16)


def _cumsum_lanes(a, lane):
    s = 1
    while s < a.shape[1]:
        a = a + jnp.where(lane >= s, pltpu.roll(a, s, axis=1), 0.0)
        s *= 2
    return a


def _zero_after(x):
    bits = lax.shift_right_logical(lax.shift_right_logical(pltpu.bitcast(x, jnp.int32), 16), 16)
    return bits.astype(F32)


def _conv31_copies(vbuf, shbuf, rows, first_blk):
    span = rows + CONF_HALO - SUBLANES
    for j in range(shbuf.shape[0]):
        cs = slice((first_blk + j) * LANES, (first_blk + j + 1) * LANES)
        for r in range(1, SUBLANES):
            shbuf[j, r - 1] = vbuf[pl.ds(r, span), cs]


def _conv31_taps(vbuf, shbuf, cw_ref, cb_ref, cbuf, rows, first_blk):
    first = CONF_HALO - (CONF_KERNEL - 1)

    def lane_block(j, carry):
        cs = pl.ds(pl.multiple_of((first_blk + j) * LANES, LANES), LANES)
        for rb in range(0, rows, CONV_ROWS):
            acc = jnp.broadcast_to(cb_ref[0:1, cs], (CONV_ROWS, LANES))
            for k in range(CONF_KERNEL):
                q, r = divmod(first + k, SUBLANES)
                lo = q * SUBLANES + rb
                win = shbuf[j, r - 1, pl.ds(lo, CONV_ROWS), :] if r else vbuf[pl.ds(lo, CONV_ROWS), cs]
                acc = acc + win * cw_ref[k:k + 1, cs]
            cbuf[rb:rb + CONV_ROWS, cs] = acc
        return carry

    lax.fori_loop(0, shbuf.shape[0], lane_block, 0)


def _mixer_kernel(x_ref, g_ref, wz_ref, wxbc_ref, wdt_ref, wcv_ref, wcg_ref,
                  scw_ref, scb_ref, dtb_ref, aneg_ref, dskip_ref, sng_ref,
                  cw_ref, cb_ref, lng_ref, lnb_ref, wout_ref, e128_ref, e64_ref,
                  out_ref, xbuf, vbuf, shbuf, cbuf, ybuf, state):
    t = pl.program_id(1)

    @pl.when(t == 0)
    def _():
        xbuf[:, TL:TL + SSD_HALO, :] = jnp.zeros((SSD_CONV - 1, SSD_HALO, XBC_WIDTH), F32)
        vbuf[0:CONF_HALO, :] = jnp.zeros((CONF_HALO, CONF_WIDTH), F32)
        state[...] = jnp.zeros(state.shape, F32)

    x = x_ref[...]
    ub = _rmsnorm(x, g_ref[...]).astype(BF16)

    vbuf[CONF_HALO:CONF_HALO + TL, :] = _dot(ub, wcv_ref[...]) * _sigmoid(_dot(ub, wcg_ref[...]))
    half_blocks = shbuf.shape[0]
    _conv31_copies(vbuf, shbuf, TL, 0)
    _conv31_taps(vbuf, shbuf, cw_ref, cb_ref, cbuf, TL, 0)
    _conv31_copies(vbuf, shbuf, TL, half_blocks)

    pre = _dot(ub, wxbc_ref[...])
    conv = scb_ref[...] + scw_ref[SSD_CONV - 1:SSD_CONV, :] * pre
    for k in range(SSD_CONV - 1):
        xbuf[k, 0:SSD_HALO, :] = xbuf[k, TL:TL + SSD_HALO, :]
        xbuf[k, pl.ds(SSD_CONV - 1 - k, TL), :] = pre
        conv = conv + scw_ref[k:k + 1, :] * xbuf[k, 0:TL, :]
    xbc = _silu(conv)

    dt = _softplus(_dot(ub, wdt_ref[...]) + dtb_ref[...])

    lane = lax.broadcasted_iota(jnp.int32, (CHUNK, LANES), 1)
    row = lax.broadcasted_iota(jnp.int32, (CHUNK, LANES), 0)
    causal = row >= lane
    low_half = lane < SSD_HEAD_DIM
    lane_h = lax.broadcasted_iota(jnp.int32, (SSD_HEADS, CHUNK), 1)

    z = _dot(ub, wz_ref[...])

    for c in range(TL // CHUNK):
        r0 = c * CHUNK
        xs = xbc[r0:r0 + CHUNK, :SSD_WIDTH]
        xs_bf = xs.astype(BF16)
        b_bf = xbc[r0:r0 + CHUNK, SSD_WIDTH:SSD_WIDTH + SSD_GROUPS * SSD_STATE].astype(BF16)
        c_bf = xbc[r0:r0 + CHUNK, SSD_WIDTH + SSD_GROUPS * SSD_STATE:].astype(BF16)

        dt_t = dt[r0:r0 + CHUNK, :].T[0:SSD_HEADS]
        acs_t = _cumsum_lanes(dt_t * aneg_ref[...], lane_h)
        acs2_t = acs_t * LOG2E
        rows_t = acs2_t - jnp.log2(dt_t)
        w_t = dt_t * jnp.exp(acs_t[:, CHUNK - 1:CHUNK] - acs_t)

        colb = _dot(_split3_cols(acs2_t), e128_ref[...])
        w64 = _dot(_split3_cols(w_t), e64_ref[...])

        e64_blocks = []
        y_blocks = []
        for g in range(SSD_GROUPS):
            bg = b_bf[:, g * SSD_STATE:(g + 1) * SSD_STATE]
            cg = c_bf[:, g * SSD_STATE:(g + 1) * SSD_STATE]
            scores = lax.dot_general(cg, bg, (((1,), (1,)), ((), ())),
                                     preferred_element_type=F32).astype(BF16)
            for jp in range(HEADS_PER_GROUP // 2):
                j = g * (HEADS_PER_GROUP // 2) + jp
                ms = []
                for h in (2 * j, 2 * j + 1):
                    seg = colb[:, h * LANES:(h + 1) * LANES] - rows_t[h:h + 1, :]
                    lmat = jnp.where(causal, jnp.exp2(seg), 0.0).astype(BF16)
                    ms.append(lmat * scores)
                xp = xs_bf[:, j * LANES:(j + 1) * LANES]
                zero = jnp.zeros_like(xp)
                rhs = jnp.concatenate([jnp.where(low_half, xp, zero),
                                       jnp.where(low_half, zero, xp)], axis=0)
                y_blocks.append(_dot(jnp.concatenate(ms, axis=1), rhs))
                e64_blocks.append(jnp.exp2(jnp.where(
                    low_half, colb[:, (2 * j) * LANES:(2 * j + 1) * LANES],
                    colb[:, (2 * j + 1) * LANES:(2 * j + 2) * LANES])))
        e64 = jnp.concatenate(e64_blocks, axis=1)
        if c == Z_BEFORE_CHUNK:
            y_blocks[0] = (y_blocks[0] + _zero_after(z[TL - CHUNK:, :LANES])
                           + _zero_after(z[TL - CHUNK:, -LANES:]))
        y_diag = jnp.concatenate(y_blocks, axis=1)

        xd_bf = (xs * w64).astype(BF16)
        y_off = []
        for g in range(SSD_GROUPS):
            gs = slice(g * GROUP_WIDTH, (g + 1) * GROUP_WIDTH)
            bg = b_bf[:, g * SSD_STATE:(g + 1) * SSD_STATE]
            cg = c_bf[:, g * SSD_STATE:(g + 1) * SSD_STATE]
            s_old = state[g]
            y_off.append(_dot(cg, s_old.astype(BF16)) * e64[:, gs])
            new = lax.dot_general(bg, xd_bf[:, gs], (((0,), (0,)), ((), ())),
                                  preferred_element_type=F32)
            state[g] = s_old * e64[CHUNK - 1:CHUNK, gs] + new
        ybuf[r0:r0 + CHUNK, :] = y_diag + jnp.concatenate(y_off, axis=1) + xs * dskip_ref[...]

    v = ybuf[...] * _silu(z)
    parts = []
    for g in range(SSD_GROUPS):
        vg = v[:, g * GROUP_WIDTH:(g + 1) * GROUP_WIDTH]
        parts.append(vg * lax.rsqrt(jnp.mean(vg * vg, axis=-1, keepdims=True) + EPS))
    y_ssd = (jnp.concatenate(parts, axis=1) * sng_ref[...]).astype(BF16)

    _conv31_taps(vbuf, shbuf, cw_ref, cb_ref, cbuf, TL, half_blocks)
    vbuf[0:CONF_HALO, :] = vbuf[TL:TL + CONF_HALO, :]

    cv = cbuf[...]
    mu = jnp.mean(cv, axis=-1, keepdims=True)
    xc = cv - mu
    ln = xc * lax.rsqrt(jnp.mean(xc * xc, axis=-1, keepdims=True) + EPS) * lng_ref[...] + lnb_ref[...]
    y_conf = _silu(ln).astype(BF16)

    out_ref[...] = (x + _dot(y_ssd, wout_ref[0:SSD_WIDTH, :])
                    + _dot(y_conf, wout_ref[SSD_WIDTH:, :]))


def _mlp_kernel(h_ref, p_ref, g1_ref, wup_ref, wdown_ref, g2_ref, wgate_ref, bgate_ref,
                wple_ref, g3_ref, gf_ref, out_ref):
    h = h_ref[...]
    u = _rmsnorm(h, g1_ref[...]).astype(BF16)
    pre = _dot(u, wup_ref[...])
    pe = p_ref[...] + _zero_after(pre[:, :PLE_DIM])
    emb = _rmsnorm(_dot(pe.astype(BF16), wple_ref[...]), g3_ref[...])
    hid = jnp.maximum(pre, 0.0)
    hid = jnp.concatenate([hid[:, :LANES] + _zero_after(emb[:, :LANES]), hid[:, LANES:]], axis=1)
    h = h + _dot((hid * hid).astype(BF16), wdown_ref[...])
    u = _rmsnorm(h, g2_ref[...]).astype(BF16)
    gate = _sigmoid(_dot(u, wgate_ref[...]) + bgate_ref[...])
    out_ref[...] = _rmsnorm(h + gate * emb, gf_ref[...])


def _const_spec(shape):
    nd = len(shape)
    return pl.BlockSpec(shape, lambda *_: (0,) * nd, pipeline_mode=pl.Buffered(1))


def _expand_matrix(width):
    e = np.zeros((LANES, SSD_HEADS * width), np.float32)
    for piece in range(SPLIT):
        for h in range(SSD_HEADS):
            e[piece * SSD_HEADS + h, h * width:(h + 1) * width] = 1.0
    return jnp.asarray(e, BF16)


def _mixer(x, mix_norm_g, w_in, ssd_conv_w, ssd_conv_b, dt_bias, A_log, D_skip, ssd_norm_g,
           conf_dw_w, conf_dw_b, conf_ln_g, conf_ln_b, w_out):
    b, l, d = x.shape
    o = np.cumsum([SSD_WIDTH, XBC_WIDTH, SSD_HEADS, CONF_WIDTH]).tolist()
    wb = w_in.astype(BF16)
    w_z, w_xbc, w_dt, w_cv, w_cg = (wb[:, :o[0]], wb[:, o[0]:o[1]], wb[:, o[1]:o[2]],
                                    wb[:, o[2]:o[3]], wb[:, o[3]:])
    pad_heads = lambda v: jnp.pad(v, ((0, 0), (0, LANES - SSD_HEADS)))
    row = lambda v: v.astype(F32)[None, :]
    consts = [
        row(mix_norm_g), w_z, w_xbc, pad_heads(w_dt), w_cv, w_cg,
        ssd_conv_w.astype(F32), row(ssd_conv_b), pad_heads(row(dt_bias)),
        -jnp.exp(A_log.astype(F32))[:, None],
        row(jnp.repeat(D_skip, SSD_HEAD_DIM)), row(ssd_norm_g),
        conf_dw_w.astype(F32), row(conf_dw_b), row(conf_ln_g), row(conf_ln_b), w_out.astype(BF16),
        _expand_matrix(LANES), _expand_matrix(SSD_HEAD_DIM),
    ]
    tile = pl.BlockSpec((None, TL, d), lambda bi, ti: (bi, ti, 0))
    return pl.pallas_call(
        _mixer_kernel,
        grid=(b, l // TL),
        in_specs=[tile] + [_const_spec(c.shape) for c in consts],
        out_specs=tile,
        out_shape=jax.ShapeDtypeStruct((b, l, d), F32),
        scratch_shapes=[
            pltpu.VMEM((SSD_CONV - 1, TL + SSD_HALO, XBC_WIDTH), F32),
            pltpu.VMEM((CONF_HALO + TL, CONF_WIDTH), F32),
            pltpu.VMEM((CONF_WIDTH // LANES // 2, SUBLANES - 1, TL + CONF_HALO - SUBLANES, LANES), F32),
            pltpu.VMEM((TL, CONF_WIDTH), F32),
            pltpu.VMEM((TL, SSD_WIDTH), F32),
            pltpu.VMEM((SSD_GROUPS, SSD_STATE, GROUP_WIDTH), F32),
        ],
        compiler_params=pltpu.CompilerParams(
            dimension_semantics=("arbitrary", "arbitrary"), vmem_limit_bytes=VMEM_LIMIT),
        name="mixer",
    )(x, *consts)


def _mlp(h, p, mlp_norm_g, w_up, w_down, ple_gate_norm_g, w_ple_gate, b_ple_gate, w_ple,
         ple_norm_g, final_norm_g):
    n, d = h.shape
    row = lambda v: v.astype(F32)[None, :]
    consts = [row(mlp_norm_g), w_up.astype(BF16), w_down.astype(BF16), row(ple_gate_norm_g),
              w_ple_gate.astype(BF16), row(b_ple_gate), w_ple.astype(BF16), row(ple_norm_g),
              row(final_norm_g)]
    return pl.pallas_call(
        _mlp_kernel,
        grid=(n // TM,),
        in_specs=[pl.BlockSpec((TM, d), lambda i: (i, 0)),
                  pl.BlockSpec((TM, PLE_DIM), lambda i: (i, 0))]
                 + [_const_spec(c.shape) for c in consts],
        out_specs=pl.BlockSpec((TM, d), lambda i: (i, 0)),
        out_shape=jax.ShapeDtypeStruct((n, d), F32),
        compiler_params=pltpu.CompilerParams(
            dimension_semantics=("arbitrary",), vmem_limit_bytes=VMEM_LIMIT),
        name="mlp",
    )(h, p, *consts)


def kernel(x, p, mix_norm_g, w_in, ssd_conv_w, ssd_conv_b, dt_bias, A_log, D_skip, ssd_norm_g,
           conf_dw_w, conf_dw_b, conf_ln_g, conf_ln_b, w_out, mlp_norm_g, w_up, w_down,
           ple_gate_norm_g, w_ple_gate, b_ple_gate, w_ple, ple_norm_g, final_norm_g):
    b, l, d = x.shape
    assert (d, l % TL, (b * l) % TM, mix_norm_g.shape[0]) == (D_MODEL, 0, 0, 1)
    h = _mixer(x, mix_norm_g[0], w_in[0], ssd_conv_w[0], ssd_conv_b[0], dt_bias[0], A_log[0],
               D_skip[0], ssd_norm_g[0], conf_dw_w[0], conf_dw_b[0], conf_ln_g[0], conf_ln_b[0],
               w_out[0])
    out = _mlp(h.reshape(b * l, d), p[0].reshape(b * l, PLE_DIM), mlp_norm_g[0], w_up[0],
               w_down[0], ple_gate_norm_g[0], w_ple_gate[0], b_ple_gate[0], w_ple[0],
               ple_norm_g[0], final_norm_g)
    return out.reshape(b, l, d)
```
